```python
import jax, jax.numpy as jnp
from jax import lax
import numpy as np

D_MODEL = 2048
BATCH = 16
SEQ = 256
DEPTH = 2
DEC_BATCH = 4
DEC_SEQ = 4096
PAST_LEN = 512

GRID_W = 64
N_HEADS = 16
HEAD_DIM = 64
D_NA = N_HEADS * HEAD_DIM
D_CONV = 1024
CONV_WIDTH = 31
WIN_ROWS = 8
WIN_COLS = 16
D_FF = 5504
N_MOD = 9
FFN_RES = 0.5
EPS = 1e-6
IN_SPLITS = (2 * D_CONV, 2 * D_CONV + D_NA, 2 * D_CONV + 2 * D_NA,
             2 * D_CONV + 3 * D_NA, 2 * D_CONV + 3 * D_NA + D_MODEL)
IN_WIDTH = 2 * D_CONV + 3 * D_NA + 2 * D_MODEL

kernel_name = "hybrid_conv_natten_diffusion_step"


def rmsnorm(x, g):
    x32 = x.astype(jnp.float32)
    y = x32 * lax.rsqrt(jnp.mean(x32 * x32, axis=-1, keepdims=True) + EPS)
    return y.astype(x.dtype) * g


def layernorm(x, g, b):
    x32 = x.astype(jnp.float32)
    mu = jnp.mean(x32, axis=-1, keepdims=True)
    var = jnp.mean(jnp.square(x32 - mu), axis=-1, keepdims=True)
    y = (x32 - mu) * lax.rsqrt(var + EPS)
    return y.astype(x.dtype) * g + b


def modulate(x, shift, scale):
    return x * (1 + scale) + shift


def ffn_sublayer(x, shift, scale, gate, g, w_gate, w_up, w_down):
    h = modulate(rmsnorm(x, g), shift, scale)
    return x + FFN_RES * gate * ((jax.nn.silu(h @ w_gate) * (h @ w_up)) @ w_down)


def in_projection(u, w_in):
    z = u @ w_in
    conv_in, q, k, v, g_conv, g_na = jnp.split(z, IN_SPLITS, axis=-1)
    b, t = u.shape[0], u.shape[1]
    q = q.reshape(b, t, N_HEADS, HEAD_DIM)
    k = k.reshape(b, t, N_HEADS, HEAD_DIM)
    v = v.reshape(b, t, N_HEADS, HEAD_DIM)
    return conv_in, q, k, v, g_conv, g_na


def conv_branch(conv_in, w_dw, b_dw, ln_g, ln_b):
    a, g = jnp.split(conv_in, 2, axis=-1)
    h = a * jax.nn.sigmoid(g)
    pad = CONV_WIDTH // 2
    h = lax.conv_general_dilated(h, w_dw[:, None, :], window_strides=(1,),
                                 padding=[(pad, pad)],
                                 dimension_numbers=('NWC', 'WIO', 'NWC'),
                                 feature_group_count=D_CONV) + b_dw
    return jax.nn.silu(layernorm(h, ln_g, ln_b))


def context_attention(q, k, v):
    b, t = q.shape[0], q.shape[1]
    s = jnp.einsum('bqhd,bkhd->bhqk', q, k).astype(jnp.float32) * (HEAD_DIM ** -0.5)
    p = jax.nn.softmax(s, axis=-1).astype(q.dtype)
    return jnp.einsum('bhqk,bkhd->bqhd', p, v).reshape(b, t, D_NA)


def neighbourhood_attention(q, k, v, ck, cv, rel_bias):
    b, t, h, dh = q.shape
    rows = t // GRID_W
    kr = min(WIN_ROWS, rows)
    kc = WIN_COLS
    qg = q.reshape(b, rows, GRID_W, h, dh)
    kg = k.reshape(b, rows, GRID_W, h, dh)
    vg = v.reshape(b, rows, GRID_W, h, dh)
    col = jnp.arange(GRID_W)
    col_start = jnp.clip(col - kc // 2, 0, GRID_W - kc)
    col_idx = col_start[:, None] + jnp.arange(kc)[None, :]
    dcol = col_idx - col[:, None] + (WIN_COLS - 1)
    scale = HEAD_DIM ** -0.5
    n_loc = kr * kc

    def row_block(r):
        r_start = jnp.clip(r - kr // 2, 0, rows - kr)
        q_r = lax.dynamic_index_in_dim(qg, r, axis=1, keepdims=False)
        k_band = lax.dynamic_slice_in_dim(kg, r_start, kr, axis=1)
        v_band = lax.dynamic_slice_in_dim(vg, r_start, kr, axis=1)
        k_win = k_band[:, :, col_idx]
        v_win = v_band[:, :, col_idx]
        drow = r_start + jnp.arange(kr) - r + (WIN_ROWS - 1)
        bias = rel_bias[:, drow[:, None, None], dcol[None, :, :]]
        bias = jnp.transpose(bias, (0, 2, 1, 3))
        s_loc = jnp.einsum('bwhd,biwjhd->bhwij', q_r, k_win).astype(jnp.float32) * scale
        s_loc = (s_loc + bias[None].astype(jnp.float32)).reshape(b, h, GRID_W, n_loc)
        s_ctx = jnp.einsum('bwhd,blhd->bhwl', q_r, ck).astype(jnp.float32) * scale
        p = jax.nn.softmax(jnp.concatenate([s_loc, s_ctx], axis=-1), axis=-1).astype(q.dtype)
        p_loc = p[..., :n_loc].reshape(b, h, GRID_W, kr, kc)
        p_ctx = p[..., n_loc:]
        return (jnp.einsum('bhwij,biwjhd->bwhd', p_loc, v_win)
                + jnp.einsum('bhwl,blhd->bwhd', p_ctx, cv))

    out = lax.map(row_block, jnp.arange(rows))
    return jnp.transpose(out, (1, 0, 2, 3, 4)).reshape(b, t, D_NA)


def merge_branches(conv_h, na_o, g_conv, g_na, w_conv_out, w_na_out, w_out):
    m = (jax.nn.sigmoid(g_conv) * (conv_h @ w_conv_out)
         + jax.nn.sigmoid(g_na) * (na_o @ w_na_out))
    return m @ w_out


def setup_inputs(seed: int = 0) -> dict:
    key = jax.random.key(seed)
    ks = jax.random.split(key, 24)
    D = D_MODEL

    def nrm(k, shape, s):
        return jax.random.normal(k, shape, jnp.float32) * s

    return {
        "x_prompt": nrm(ks[0], (BATCH, SEQ, D), 1.0),
        "x_sample": nrm(ks[1], (DEC_BATCH, DEC_SEQ, D), 1.0),
        "cache_k": nrm(ks[2], (DEC_BATCH, DEPTH, PAST_LEN, N_HEADS, HEAD_DIM), 1.0),
        "cache_v": nrm(ks[3], (DEC_BATCH, DEPTH, PAST_LEN, N_HEADS, HEAD_DIM), 1.0),
        "c": nrm(ks[4], (DEC_BATCH, D), 1.0),
        "c_ctx": nrm(ks[5], (D,), 1.0),
        "w_ada": nrm(ks[6], (DEPTH, D, N_MOD * D), 0.5 * D ** -0.5),
        "b_ada": nrm(ks[7], (DEPTH, N_MOD * D), 0.01),
        "norm_g": 1.0 + nrm(ks[8], (DEPTH, 3, D), 0.02),
        "ffn_w_gate": nrm(ks[9], (DEPTH, 2, D, D_FF), D ** -0.5),
        "ffn_w_up": nrm(ks[10], (DEPTH, 2, D, D_FF), D ** -0.5),
        "ffn_w_down": nrm(ks[11], (DEPTH, 2, D_FF, D), D_FF ** -0.5),
        "w_in": nrm(ks[12], (DEPTH, D, IN_WIDTH), D ** -0.5),
        "conv_w": nrm(ks[13], (DEPTH, CONV_WIDTH, D_CONV), CONV_WIDTH ** -0.5),
        "conv_b": nrm(ks[14], (DEPTH, D_CONV), 0.01),
        "conv_ln_g": 1.0 + nrm(ks[15], (DEPTH, D_CONV), 0.02),
        "conv_ln_b": nrm(ks[16], (DEPTH, D_CONV), 0.01),
        "w_conv_out": nrm(ks[17], (DEPTH, D_CONV, D), D_CONV ** -0.5),
        "rel_bias": nrm(ks[18], (DEPTH, N_HEADS, 2 * WIN_ROWS - 1, 2 * WIN_COLS - 1), 0.1),
        "w_na_out": nrm(ks[19], (DEPTH, D_NA, D), D_NA ** -0.5),
        "w_out": nrm(ks[20], (DEPTH, D, D), D ** -0.5),
        "final_g": 1.0 + nrm(ks[21], (D,), 0.02),
    }


def reference(x_prompt, x_sample, cache_k, cache_v, c, c_ctx, w_ada, b_ada, norm_g,
              ffn_w_gate, ffn_w_up, ffn_w_down, w_in, conv_w, conv_b, conv_ln_g,
              conv_ln_b, w_conv_out, rel_bias, w_na_out, w_out, final_g):
    y_p = x_prompt
    y_s = x_sample
    silu_ctx = jax.nn.silu(c_ctx)
    silu_c = jax.nn.silu(c)
    keys_out = []
    vals_out = []
    for l in range(DEPTH):
        mod_p = jnp.split((silu_ctx @ w_ada[l] + b_ada[l])[None, None, :], N_MOD, axis=-1)
        mod_s = jnp.split((silu_c @ w_ada[l] + b_ada[l])[:, None, :], N_MOD, axis=-1)

        y_p = ffn_sublayer(y_p, mod_p[0], mod_p[1], mod_p[2], norm_g[l, 0],
                           ffn_w_gate[l, 0], ffn_w_up[l, 0], ffn_w_down[l, 0])
        y_s = ffn_sublayer(y_s, mod_s[0], mod_s[1], mod_s[2], norm_g[l, 0],
                           ffn_w_gate[l, 0], ffn_w_up[l, 0], ffn_w_down[l, 0])

        u_p = modulate(rmsnorm(y_p, norm_g[l, 1]), mod_p[3], mod_p[4])
        conv_in_p, q_p, k_p, v_p, gc_p, gn_p = in_projection(u_p, w_in[l])
        keys_out.append(k_p)
        vals_out.append(v_p)
        conv_h_p = conv_branch(conv_in_p, conv_w[l], conv_b[l], conv_ln_g[l], conv_ln_b[l])
        na_p = context_attention(q_p, k_p, v_p)
        y_p = y_p + mod_p[5] * merge_branches(conv_h_p, na_p, gc_p, gn_p,
                                              w_conv_out[l], w_na_out[l], w_out[l])

        u_s = modulate(rmsnorm(y_s, norm_g[l, 1]), mod_s[3], mod_s[4])
        conv_in_s, q_s, k_s, v_s, gc_s, gn_s = in_projection(u_s, w_in[l])
        conv_h_s = conv_branch(conv_in_s, conv_w[l], conv_b[l], conv_ln_g[l], conv_ln_b[l])
        na_s = neighbourhood_attention(q_s, k_s, v_s, cache_k[:, l], cache_v[:, l], rel_bias[l])
        y_s = y_s + mod_s[5] * merge_branches(conv_h_s, na_s, gc_s, gn_s,
                                              w_conv_out[l], w_na_out[l], w_out[l])

        y_p = ffn_sublayer(y_p, mod_p[6], mod_p[7], mod_p[8], norm_g[l, 2],
                           ffn_w_gate[l, 1], ffn_w_up[l, 1], ffn_w_down[l, 1])
        y_s = ffn_sublayer(y_s, mod_s[6], mod_s[7], mod_s[8], norm_g[l, 2],
                           ffn_w_gate[l, 1], ffn_w_up[l, 1], ffn_w_down[l, 1])

    y_prompt = rmsnorm(y_p, final_g)
    y_sample = rmsnorm(y_s, final_g)
    new_k = jnp.stack(keys_out, axis=1)
    new_v = jnp.stack(vals_out, axis=1)
    return (y_prompt, y_sample, new_k, new_v)
```

```python
import functools

import jax
import jax.numpy as jnp
from jax import lax
from jax.experimental import pallas as pl
from jax.experimental.pallas import tpu as pltpu

F32 = jnp.float32
BF16 = jnp.bfloat16

N_HEADS = 16
HEAD_DIM = 64
GRID_W = 64
WIN_ROWS = 8
WIN_COLS = 16
CONV_WIDTH = 31
N_MOD = 9
FFN_RES = 0.5
EPS = 1e-6
NEG_INF = -1e30

MOD_ROWS = 8
HALO = 16
VMEM_LIMIT = 56 * 1024 * 1024


def _params(sem, vmem=VMEM_LIMIT):
    return pltpu.CompilerParams(dimension_semantics=sem, vmem_limit_bytes=vmem)


def _silu(x):
    return x * jax.nn.sigmoid(x)


def _ada_kernel(c_ref, w_ref, b_ref, o_ref):
    s = _silu(c_ref[...]).astype(BF16)
    o_ref[...] = jnp.dot(s, w_ref[...].astype(BF16), preferred_element_type=F32) + b_ref[...]


def _ada_table(cvec, w_ada, b_ada):
    depth, d, n = w_ada.shape
    tn = min(1024, d)
    assert n % tn == 0
    return pl.pallas_call(
        _ada_kernel,
        grid=(depth, n // tn),
        in_specs=[
            pl.BlockSpec((MOD_ROWS, d), lambda l, j: (0, 0)),
            pl.BlockSpec((None, d, tn), lambda l, j: (l, 0, j)),
            pl.BlockSpec((None, 1, tn), lambda l, j: (l, 0, j)),
        ],
        out_specs=pl.BlockSpec((None, MOD_ROWS, tn), lambda l, j: (l, 0, j)),
        out_shape=jax.ShapeDtypeStruct((depth, MOD_ROWS, n), F32),
        compiler_params=_params(("arbitrary", "arbitrary")),
        name="ada_table",
    )(cvec, w_ada, b_ada.reshape(depth, 1, n))


def _mod_spec(layer, m, tm, d, n_prompt, dec_seq):
    def idx(i, *_):
        row0 = i * tm
        grp = jnp.where(row0 < n_prompt, 0, 1 + (row0 - n_prompt) // dec_seq)
        return (layer, grp, m, 0, 0)
    return pl.BlockSpec((None, None, None, 1, d), idx)


def _norm_mod(x, g, shift, scale):
    ms = jnp.mean(x * x, axis=-1, keepdims=True)
    return (x * lax.rsqrt(ms + EPS)) * g * (1.0 + scale) + shift


def _ffn_kernel(x_ref, shift_ref, scale_ref, gate_ref, g_ref, wg_ref, wu_ref, wd_ref, o_ref, hn_ref):
    j = pl.program_id(1)

    @pl.when(j == 0)
    def _():
        hn_ref[...] = _norm_mod(x_ref[...], g_ref[...], shift_ref[...], scale_ref[...]).astype(BF16)

    hn = hn_ref[...]
    a = jnp.dot(hn, wg_ref[...], preferred_element_type=F32)
    u = jnp.dot(hn, wu_ref[...], preferred_element_type=F32)
    h = (_silu(a) * u).astype(BF16)
    part = jnp.dot(h, wd_ref[...], preferred_element_type=F32)

    @pl.when(j == 0)
    def _():
        o_ref[...] = part

    @pl.when(j > 0)
    def _():
        o_ref[...] += part

    @pl.when(j == pl.num_programs(1) - 1)
    def _():
        o_ref[...] = x_ref[...] + FFN_RES * gate_ref[...] * o_ref[...]


def _ffn(x, mod, norm_g, wg, wu, wd, layer, sub, mod0, n_prompt, dec_seq, tm=512, tf=512):
    m_tok, d = x.shape
    ffp = wg.shape[-1]
    mspec = functools.partial(_mod_spec, layer, tm=tm, d=d, n_prompt=n_prompt, dec_seq=dec_seq)
    return pl.pallas_call(
        _ffn_kernel,
        grid=(m_tok // tm, ffp // tf),
        in_specs=[
            pl.BlockSpec((tm, d), lambda i, j: (i, 0)),
            mspec(m=mod0), mspec(m=mod0 + 1), mspec(m=mod0 + 2),
            pl.BlockSpec((None, None, 1, d), lambda i, j: (layer, 2 * sub, 0, 0)),
            pl.BlockSpec((None, None, d, tf), lambda i, j: (layer, sub, 0, j)),
            pl.BlockSpec((None, None, d, tf), lambda i, j: (layer, sub, 0, j)),
            pl.BlockSpec((None, None, tf, d), lambda i, j: (layer, sub, j, 0)),
        ],
        out_specs=pl.BlockSpec((tm, d), lambda i, j: (i, 0)),
        out_shape=jax.ShapeDtypeStruct((m_tok, d), F32),
        scratch_shapes=[pltpu.VMEM((tm, d), BF16)],
        compiler_params=_params(("parallel", "arbitrary")),
        name=f"ffn_l{layer}_s{sub}",
    )(x, mod, mod, mod, norm_g, wg, wu, wd)


def _inproj_kernel(x_ref, shift_ref, scale_ref, g_ref, w_ref, o_ref, hn_ref):
    @pl.when(pl.program_id(1) == 0)
    def _():
        hn_ref[...] = _norm_mod(x_ref[...], g_ref[...], shift_ref[...], scale_ref[...]).astype(BF16)

    o_ref[...] = jnp.dot(hn_ref[...], w_ref[...], preferred_element_type=F32).astype(o_ref.dtype)


def _inproj(x, mod, norm_g, w_in, layer, col0, ncols, out_dtype, n_prompt, dec_seq, tm=512, tn=512):
    m_tok, d = x.shape
    mspec = functools.partial(_mod_spec, layer, tm=tm, d=d, n_prompt=n_prompt, dec_seq=dec_seq)
    cb0 = col0 // tn
    return pl.pallas_call(
        _inproj_kernel,
        grid=(m_tok // tm, ncols // tn),
        in_specs=[
            pl.BlockSpec((tm, d), lambda i, j: (i, 0)),
            mspec(m=3), mspec(m=4),
            pl.BlockSpec((None, None, 1, d), lambda i, j: (layer, 1, 0, 0)),
            pl.BlockSpec((None, d, tn), lambda i, j: (layer, 0, cb0 + j)),
        ],
        out_specs=pl.BlockSpec((tm, tn), lambda i, j: (i, j)),
        out_shape=jax.ShapeDtypeStruct((m_tok, ncols), out_dtype),
        scratch_shapes=[pltpu.VMEM((tm, d), BF16)],
        compiler_params=_params(("parallel", "arbitrary")),
        name=f"inproj_l{layer}_c{col0}",
    )(x, mod, mod, norm_g, w_in)


def _conv_kernel(prev_ref, cur_ref, next_ref, w_ref, b_ref, lg_ref, lb_ref, o_ref, hs_ref, cv_ref,
                 *, tp, dc, prompt_tiles, tiles_per_seq):
    i = pl.program_id(0)

    def glu(ref):
        z = ref[...].astype(F32)
        return z[:, :dc] * jax.nn.sigmoid(z[:, dc:])

    pos = lax.rem(i - prompt_tiles, tiles_per_seq)
    is_latent = i >= prompt_tiles
    has_prev = jnp.logical_and(is_latent, pos != 0)
    has_next = jnp.logical_and(is_latent, pos != tiles_per_seq - 1)
    hs_ref[0:HALO, :] = jnp.where(has_prev, glu(prev_ref), 0.0)
    hs_ref[HALO:HALO + tp, :] = glu(cur_ref)
    hs_ref[HALO + tp:2 * HALO + tp, :] = jnp.where(has_next, glu(next_ref), 0.0)

    pad = CONV_WIDTH // 2
    rc = 32
    for c0 in range(0, dc, 128):
        for r0 in range(0, tp, rc):
            acc = jnp.zeros((rc, 128), F32)
            for k in range(CONV_WIDTH):
                start = HALO - pad + r0 + k
                acc = acc + hs_ref[start:start + rc, c0:c0 + 128] * w_ref[k:k + 1, c0:c0 + 128]
            cv_ref[r0:r0 + rc, c0:c0 + 128] = acc

    h = cv_ref[...] + b_ref[...]
    mu = jnp.mean(h, axis=-1, keepdims=True)
    hc = h - mu
    var = jnp.mean(hc * hc, axis=-1, keepdims=True)
    y = hc * lax.rsqrt(var + EPS) * lg_ref[...] + lb_ref[...]
    o_ref[...] = _silu(y).astype(o_ref.dtype)


def _conv_branch(a, conv_w, conv_b, ln_g, ln_b, layer, n_prompt, seq, dec_seq, tp=256):
    m_tok = a.shape[0]
    dc = conv_w.shape[-1]
    assert seq == tp and dec_seq % tp == 0
    hb = tp // HALO
    n_hblk = m_tok // HALO
    kern = functools.partial(_conv_kernel, tp=tp, dc=dc, prompt_tiles=n_prompt // tp,
                             tiles_per_seq=dec_seq // tp)
    vec = pl.BlockSpec((None, 1, dc), lambda i: (layer, 0, 0))
    return pl.pallas_call(
        kern,
        grid=(m_tok // tp,),
        in_specs=[
            pl.BlockSpec((HALO, 2 * dc), lambda i: (jnp.maximum(i * hb - 1, 0), 0)),
            pl.BlockSpec((tp, 2 * dc), lambda i: (i, 0)),
            pl.BlockSpec((HALO, 2 * dc), lambda i: (jnp.minimum((i + 1) * hb, n_hblk - 1), 0)),
            pl.BlockSpec((None, CONV_WIDTH, dc), lambda i: (layer, 0, 0)),
            vec, vec, vec,
        ],
        out_specs=pl.BlockSpec((tp, dc), lambda i: (i, 0)),
        out_shape=jax.ShapeDtypeStruct((m_tok, dc), BF16),
        scratch_shapes=[pltpu.VMEM((tp + 2 * HALO, dc), F32), pltpu.VMEM((tp, dc), F32)],
        compiler_params=_params(("parallel",)),
        name=f"conv_l{layer}",
    )(a, a, a, conv_w, conv_b.reshape(-1, 1, dc), ln_g.reshape(-1, 1, dc), ln_b.reshape(-1, 1, dc))


def _dot_nt(a, b):
    return lax.dot_general(a, b, (((1,), (1,)), ((), ())), preferred_element_type=F32)


def _ctx_attn_kernel(q_ref, k_ref, v_ref, o_ref):
    scale = HEAD_DIM ** -0.5
    outs = []
    for h in range(N_HEADS):
        sl = slice(h * HEAD_DIM, (h + 1) * HEAD_DIM)
        q = q_ref[:, sl]
        k = k_ref[:, sl].astype(BF16)
        v = v_ref[:, sl].astype(BF16)
        s = _dot_nt(q, k) * scale
        e = jnp.exp(s - jnp.max(s, axis=-1, keepdims=True))
        p = e / jnp.sum(e, axis=-1, keepdims=True)
        outs.append(jnp.dot(p.astype(BF16), v, preferred_element_type=F32))
    o_ref[...] = jnp.concatenate(outs, axis=-1).astype(o_ref.dtype)


def _ctx_attention(a, kv, batch, seq, q_col0):
    d_na = N_HEADS * HEAD_DIM
    qb = q_col0 // d_na
    return pl.pallas_call(
        _ctx_attn_kernel,
        grid=(batch,),
        in_specs=[
            pl.BlockSpec((seq, d_na), lambda b: (b, qb)),
            pl.BlockSpec((seq, d_na), lambda b: (b, 0)),
            pl.BlockSpec((seq, d_na), lambda b: (b, 1)),
        ],
        out_specs=pl.BlockSpec((seq, d_na), lambda b: (b, 0)),
        out_shape=jax.ShapeDtypeStruct((batch * seq, d_na), BF16),
        compiler_params=_params(("parallel",)),
        name="ctx_attn",
    )(a, kv, kv)


def _na_kernel(q_ref, k_ref, v_ref, ck_ref, cv_ref, bias_ref, o_ref, *, rb, rows):
    scale = HEAD_DIM ** -0.5
    band = WIN_ROWS * GRID_W
    row0 = pl.program_id(2) * rb
    for rr in range(rb):
        r = row0 + rr
        r_start = jnp.clip(r - WIN_ROWS // 2, 0, rows - WIN_ROWS)
        d0 = r_start - r + (WIN_ROWS - 1)
        k0 = pl.multiple_of(r_start * GRID_W, GRID_W)
        kb = k_ref[pl.ds(k0, band), :].astype(BF16)
        vb = v_ref[pl.ds(k0, band), :].astype(BF16)
        outs = []
        for hh in range(2):
            sl = slice(hh * HEAD_DIM, (hh + 1) * HEAD_DIM)
            q = q_ref[rr * GRID_W:(rr + 1) * GRID_W, sl]
            s_loc = _dot_nt(q, kb[:, sl]) * scale + bias_ref[hh, d0]
            s_ctx = _dot_nt(q, ck_ref[:, sl]) * scale
            m = jnp.maximum(jnp.max(s_loc, axis=-1, keepdims=True), jnp.max(s_ctx, axis=-1, keepdims=True))
            e_loc = jnp.exp(s_loc - m)
            e_ctx = jnp.exp(s_ctx - m)
            denom = jnp.sum(e_loc, axis=-1, keepdims=True) + jnp.sum(e_ctx, axis=-1, keepdims=True)
            o = (jnp.dot(e_loc.astype(BF16), vb[:, sl], preferred_element_type=F32)
                 + jnp.dot(e_ctx.astype(BF16), cv_ref[:, sl], preferred_element_type=F32))
            outs.append(o / denom)
        o_ref[rr * GRID_W:(rr + 1) * GRID_W, :] = jnp.concatenate(outs, axis=-1).astype(o_ref.dtype)


def _na_bias_table(rel_bias):
    w = jnp.arange(GRID_W)[:, None]
    j = jnp.arange(GRID_W)[None, :]
    cs = jnp.clip(w - WIN_COLS // 2, 0, GRID_W - WIN_COLS)
    valid = (j >= cs) & (j < cs + WIN_COLS)
    dcol = jnp.clip(j - w + (WIN_COLS - 1), 0, 2 * WIN_COLS - 2)
    t = jnp.where(valid[None, None], rel_bias[:, :, dcol], NEG_INF)
    d = jnp.arange(WIN_ROWS)[:, None] + jnp.arange(WIN_ROWS)[None, :]
    tb = t[:, d]
    tb = jnp.transpose(tb, (0, 1, 3, 2, 4))
    return tb.reshape(rel_bias.shape[0], WIN_ROWS, GRID_W, WIN_ROWS * GRID_W)


def _na_attention(a, kv, ck, cv, bias, layer, n_prompt, dec_batch, dec_seq, q_col0, rb=8):
    d_na = N_HEADS * HEAD_DIM
    rows = dec_seq // GRID_W
    hpw = 2 * HEAD_DIM
    qrows = rb * GRID_W
    past = ck.shape[2]
    kern = functools.partial(_na_kernel, rb=rb, rows=rows)
    q_blk0 = n_prompt // qrows
    kv_blk0 = n_prompt // dec_seq
    return pl.pallas_call(
        kern,
        grid=(N_HEADS // 2, dec_batch, rows // rb),
        in_specs=[
            pl.BlockSpec((qrows, hpw), lambda hp, b, r: (q_blk0 + b * (rows // rb) + r, q_col0 // hpw + hp)),
            pl.BlockSpec((dec_seq, hpw), lambda hp, b, r: (kv_blk0 + b, hp)),
            pl.BlockSpec((dec_seq, hpw), lambda hp, b, r: (kv_blk0 + b, d_na // hpw + hp)),
            pl.BlockSpec((None, None, past, hpw), lambda hp, b, r: (b, layer, 0, hp)),
            pl.BlockSpec((None, None, past, hpw), lambda hp, b, r: (b, layer, 0, hp)),
            pl.BlockSpec((2, WIN_ROWS, GRID_W, WIN_ROWS * GRID_W), lambda hp, b, r: (hp, 0, 0, 0)),
        ],
        out_specs=pl.BlockSpec((qrows, hpw), lambda hp, b, r: (b * (rows // rb) + r, hp)),
        out_shape=jax.ShapeDtypeStruct((dec_batch * dec_seq, d_na), BF16),
        compiler_params=_params(("parallel", "parallel", "arbitrary")),
        name=f"na_l{layer}",
    )(a, kv, kv, ck, cv, bias)


def _merge_kernel(x_ref, gate_ref, ch_ref, no_ref, gc_ref, gn_ref, wc_ref, wn_ref, wo_ref, o_ref):
    yc = jnp.dot(ch_ref[...], wc_ref[...], preferred_element_type=F32)
    yn = jnp.dot(no_ref[...], wn_ref[...], preferred_element_type=F32)
    m = (jax.nn.sigmoid(gc_ref[...].astype(F32)) * yc + jax.nn.sigmoid(gn_ref[...].astype(F32)) * yn)
    o_ref[...] = x_ref[...] + gate_ref[...] * jnp.dot(m.astype(BF16), wo_ref[...], preferred_element_type=F32)


def _merge(x, mod, conv_h, na_o, gates, w_conv_out, w_na_out, w_out, layer, n_prompt, dec_seq, tm=256):
    m_tok, d = x.shape
    dc = conv_h.shape[1]
    dn = na_o.shape[1]
    mspec = functools.partial(_mod_spec, layer, tm=tm, d=d, n_prompt=n_prompt, dec_seq=dec_seq)
    return pl.pallas_call(
        _merge_kernel,
        grid=(m_tok // tm,),
        in_specs=[
            pl.BlockSpec((tm, d), lambda i: (i, 0)),
            mspec(m=5),
            pl.BlockSpec((tm, dc), lambda i: (i, 0)),
            pl.BlockSpec((tm, dn), lambda i: (i, 0)),
            pl.BlockSpec((tm, d), lambda i: (i, 0)),
            pl.BlockSpec((tm, d), lambda i: (i, 1)),
            pl.BlockSpec((None, dc, d), lambda i: (layer, 0, 0)),
            pl.BlockSpec((None, dn, d), lambda i: (layer, 0, 0)),
            pl.BlockSpec((None, d, d), lambda i: (layer, 0, 0)),
        ],
        out_specs=pl.BlockSpec((tm, d), lambda i: (i, 0)),
        out_shape=jax.ShapeDtypeStruct((m_tok, d), F32),
        compiler_params=_params(("parallel",)),
        name=f"merge_l{layer}",
    )(x, mod, conv_h, na_o, gates, gates, w_conv_out, w_na_out, w_out)


def _final_norm_kernel(x_ref, g_ref, o_ref):
    x = x_ref[...]
    ms = jnp.mean(x * x, axis=-1, keepdims=True)
    o_ref[...] = (x * lax.rsqrt(ms + EPS)) * g_ref[...]


def _final_norm(x, g, row0, nrows, tm=512):
    d = x.shape[1]
    b0 = row0 // tm
    return pl.pallas_call(
        _final_norm_kernel,
        grid=(nrows // tm,),
        in_specs=[pl.BlockSpec((tm, d), lambda i: (b0 + i, 0)), pl.BlockSpec((1, d), lambda i: (0, 0))],
        out_specs=pl.BlockSpec((tm, d), lambda i: (i, 0)),
        out_shape=jax.ShapeDtypeStruct((nrows, d), F32),
        compiler_params=_params(("parallel",)),
        name=f"final_norm_r{row0}",
    )(x, g.reshape(1, d))


def kernel(x_prompt, x_sample, cache_k, cache_v, c, c_ctx, w_ada, b_ada, norm_g, ffn_w_gate, ffn_w_up,
           ffn_w_down, w_in, conv_w, conv_b, conv_ln_g, conv_ln_b, w_conv_out, rel_bias, w_na_out, w_out,
           final_g):
    batch, seq, d = x_prompt.shape
    dec_batch, dec_seq, _ = x_sample.shape
    depth = w_ada.shape[0]
    d_ff = ffn_w_gate.shape[-1]
    d_na = N_HEADS * HEAD_DIM
    dc = conv_w.shape[-1]
    past = cache_k.shape[2]
    n_prompt = batch * seq
    n_latent = dec_batch * dec_seq

    x = jnp.concatenate([x_prompt.reshape(n_prompt, d), x_sample.reshape(n_latent, d)], axis=0)

    cvec = jnp.concatenate([c_ctx[None, :], c, jnp.zeros((MOD_ROWS - 1 - dec_batch, d), F32)], axis=0)
    mod = _ada_table(cvec, w_ada, b_ada).reshape(depth, MOD_ROWS, N_MOD, 1, d)

    ff_pad = (-d_ff) % 512
    wg = jnp.pad(ffn_w_gate.astype(BF16), ((0, 0), (0, 0), (0, 0), (0, ff_pad)))
    wu = jnp.pad(ffn_w_up.astype(BF16), ((0, 0), (0, 0), (0, 0), (0, ff_pad)))
    wd = jnp.pad(ffn_w_down.astype(BF16), ((0, 0), (0, 0), (0, ff_pad), (0, 0)))
    w_in_b = w_in.astype(BF16)
    w_co_b = w_conv_out.astype(BF16)
    w_no_b = w_na_out.astype(BF16)
    w_o_b = w_out.astype(BF16)
    ck = cache_k.reshape(dec_batch, depth, past, d_na).astype(BF16)
    cv = cache_v.reshape(dec_batch, depth, past, d_na).astype(BF16)
    norm_g4 = norm_g.reshape(depth, 3, 1, d)

    a_cols = 2 * dc + d_na
    kv_col0 = a_cols
    g_col0 = a_cols + 2 * d_na

    keys, vals = [], []
    for l in range(depth):
        x = _ffn(x, mod, norm_g4, wg, wu, wd, l, 0, 0, n_prompt, dec_seq)

        a = _inproj(x, mod, norm_g4, w_in_b, l, 0, a_cols, BF16, n_prompt, dec_seq)
        kv = _inproj(x, mod, norm_g4, w_in_b, l, kv_col0, 2 * d_na, F32, n_prompt, dec_seq)
        gates = _inproj(x, mod, norm_g4, w_in_b, l, g_col0, 2 * d, BF16, n_prompt, dec_seq)
        keys.append(kv[:n_prompt, :d_na].reshape(batch, seq, N_HEADS, HEAD_DIM))
        vals.append(kv[:n_prompt, d_na:].reshape(batch, seq, N_HEADS, HEAD_DIM))

        conv_h = _conv_branch(a, conv_w, conv_b, conv_ln_g, conv_ln_b, l, n_prompt, seq, dec_seq)
        na_p = _ctx_attention(a, kv, batch, seq, 2 * dc)
        na_s = _na_attention(a, kv, ck, cv, _na_bias_table(rel_bias[l]), l, n_prompt, dec_batch, dec_seq,
                             2 * dc)
        na_o = jnp.concatenate([na_p, na_s], axis=0)
        x = _merge(x, mod, conv_h, na_o, gates, w_co_b, w_no_b, w_o_b, l, n_prompt, dec_seq)

        x = _ffn(x, mod, norm_g4, wg, wu, wd, l, 1, 6, n_prompt, dec_seq)

    y_prompt = _final_norm(x, final_g, 0, n_prompt).reshape(batch, seq, d)
    y_sample = _final_norm(x, final_g, n_prompt, n_latent).reshape(dec_batch, dec_seq, d)
    return (y_prompt, y_sample, jnp.stack(keys, axis=1), jnp.stack(vals, axis=1))
```

```python
import functools

import jax
import jax.numpy as jnp
from jax import lax
from jax.experimental import pallas as pl
from jax.experimental.pallas import tpu as pltpu

F32 = jnp.float32
BF16 = jnp.bfloat16

N_HEADS = 16
HEAD_DIM = 64
GRID_W = 64
WIN_ROWS = 8
WIN_COLS = 16
CONV_WIDTH = 31
N_MOD = 9
FFN_RES = 0.5
EPS = 1e-6
NEG_INF = -1e30

FFN_CHUNK = 512
INPROJ_CHUNK = 512
MOD_ROWS = 8
HALO = 16
VMEM_LIMIT = 56 * 1024 * 1024


def _params(sem, vmem=VMEM_LIMIT):
    return pltpu.CompilerParams(dimension_semantics=sem, vmem_limit_bytes=vmem)


def _silu(x):
    return x * jax.nn.sigmoid(x)


def _ada_kernel(c_ref, w_ref, b_ref, o_ref):
    s = _silu(c_ref[...]).astype(BF16)
    o_ref[...] = jnp.dot(s, w_ref[...].astype(BF16), preferred_element_type=F32) + b_ref[...]


def _ada_table(cvec, w_ada, b_ada):
    depth, d, n = w_ada.shape
    tn = min(1024, d)
    assert n % tn == 0
    return pl.pallas_call(
        _ada_kernel,
        grid=(depth, n // tn),
        in_specs=[
            pl.BlockSpec((MOD_ROWS, d), lambda l, j: (0, 0)),
            pl.BlockSpec((None, d, tn), lambda l, j: (l, 0, j)),
            pl.BlockSpec((None, 1, tn), lambda l, j: (l, 0, j)),
        ],
        out_specs=pl.BlockSpec((None, MOD_ROWS, tn), lambda l, j: (l, 0, j)),
        out_shape=jax.ShapeDtypeStruct((depth, MOD_ROWS, n), F32),
        compiler_params=_params(("arbitrary", "arbitrary")),
        name="ada_table",
    )(cvec, w_ada, b_ada.reshape(depth, 1, n))


def _mod_spec(layer, m, tm, d, n_prompt, dec_seq):
    def idx(i, *_):
        row0 = i * tm
        grp = jnp.where(row0 < n_prompt, 0, 1 + (row0 - n_prompt) // dec_seq)
        return (layer, grp, m, 0, 0)
    return pl.BlockSpec((None, None, None, 1, d), idx)


def _norm_mod(x, g, shift, scale):
    ms = jnp.mean(x * x, axis=-1, keepdims=True)
    return (x * lax.rsqrt(ms + EPS)) * g * (1.0 + scale) + shift


def _ffn_kernel(x_ref, shift_ref, scale_ref, gate_ref, g_ref, wg_ref, wu_ref, wd_ref, o_ref, hn_ref,
                *, n_split):
    j = pl.program_id(1)

    @pl.when(j == 0)
    def _():
        hn_ref[...] = _norm_mod(x_ref[...], g_ref[...], shift_ref[...], scale_ref[...]).astype(BF16)
        o_ref[...] = jnp.zeros_like(o_ref)

    hn = hn_ref[...]
    cw = wg_ref.shape[-1] // n_split
    part = None
    for c in range(n_split):
        sl = slice(c * cw, (c + 1) * cw)
        a = jnp.dot(hn, wg_ref[:, sl], preferred_element_type=F32)
        u = jnp.dot(hn, wu_ref[:, sl], preferred_element_type=F32)
        h = (_silu(a) * u).astype(BF16)
        p = jnp.dot(h, wd_ref[sl, :], preferred_element_type=F32)
        part = p if part is None else part + p
    o_ref[...] += part

    @pl.when(j == pl.num_programs(1) - 1)
    def _():
        o_ref[...] = x_ref[...] + FFN_RES * gate_ref[...] * o_ref[...]


def _ffn(x, mod, norm_g, wg, wu, wd, layer, sub, mod0, n_prompt, dec_seq, tm=512, n_split=2):
    m_tok, d = x.shape
    n_chunks, tf = wg.shape[2], wg.shape[-1]
    mspec = functools.partial(_mod_spec, layer, tm=tm, d=d, n_prompt=n_prompt, dec_seq=dec_seq)
    return pl.pallas_call(
        functools.partial(_ffn_kernel, n_split=n_split),
        grid=(m_tok // tm, n_chunks),
        in_specs=[
            pl.BlockSpec((tm, d), lambda i, j: (i, 0)),
            mspec(m=mod0), mspec(m=mod0 + 1), mspec(m=mod0 + 2),
            pl.BlockSpec((None, None, 1, d), lambda i, j: (layer, 2 * sub, 0, 0)),
            pl.BlockSpec((None, None, None, d, tf), lambda i, j: (layer, sub, j, 0, 0)),
            pl.BlockSpec((None, None, None, d, tf), lambda i, j: (layer, sub, j, 0, 0)),
            pl.BlockSpec((None, None, tf, d), lambda i, j: (layer, sub, j, 0)),
        ],
        out_specs=pl.BlockSpec((tm, d), lambda i, j: (i, 0)),
        out_shape=jax.ShapeDtypeStruct((m_tok, d), F32),
        scratch_shapes=[pltpu.VMEM((tm, d), BF16)],
        compiler_params=_params(("parallel", "arbitrary")),
        name=f"ffn_l{layer}_s{sub}",
    )(x, mod, mod, mod, norm_g, wg, wu, wd)


def _inproj_kernel(x_ref, shift_ref, scale_ref, g_ref, w_ref, a_ref, kv_ref, gt_ref, hn_ref, *, na, nkv):
    j = pl.program_id(1)

    @pl.when(j == 0)
    def _():
        hn_ref[...] = _norm_mod(x_ref[...], g_ref[...], shift_ref[...], scale_ref[...]).astype(BF16)

    z = jnp.dot(hn_ref[...], w_ref[...], preferred_element_type=F32)

    @pl.when(j < na)
    def _():
        a_ref[...] = z.astype(a_ref.dtype)

    @pl.when(jnp.logical_and(j >= na, j < na + nkv))
    def _():
        kv_ref[...] = z

    @pl.when(j >= na + nkv)
    def _():
        gt_ref[...] = z.astype(gt_ref.dtype)


def _inproj(x, mod, norm_g, w_in, layer, a_cols, kv_cols, g_cols, n_prompt, dec_seq, tm=1024):
    m_tok, d = x.shape
    n_chunks, tn = w_in.shape[1], w_in.shape[-1]
    na, nkv, ng = a_cols // tn, kv_cols // tn, g_cols // tn
    assert na + nkv + ng == n_chunks
    mspec = functools.partial(_mod_spec, layer, tm=tm, d=d, n_prompt=n_prompt, dec_seq=dec_seq)
    return pl.pallas_call(
        functools.partial(_inproj_kernel, na=na, nkv=nkv),
        grid=(m_tok // tm, n_chunks),
        in_specs=[
            pl.BlockSpec((tm, d), lambda i, j: (i, 0)),
            mspec(m=3), mspec(m=4),
            pl.BlockSpec((None, None, 1, d), lambda i, j: (layer, 1, 0, 0)),
            pl.BlockSpec((None, None, d, tn), lambda i, j: (layer, j, 0, 0)),
        ],
        out_specs=[
            pl.BlockSpec((tm, tn), lambda i, j: (i, jnp.minimum(j, na - 1))),
            pl.BlockSpec((tm, tn), lambda i, j: (i, jnp.clip(j - na, 0, nkv - 1))),
            pl.BlockSpec((tm, tn), lambda i, j: (i, jnp.clip(j - na - nkv, 0, ng - 1))),
        ],
        out_shape=[
            jax.ShapeDtypeStruct((m_tok, a_cols), BF16),
            jax.ShapeDtypeStruct((m_tok, kv_cols), F32),
            jax.ShapeDtypeStruct((m_tok, g_cols), BF16),
        ],
        scratch_shapes=[pltpu.VMEM((tm, d), BF16)],
        compiler_params=_params(("parallel", "arbitrary")),
        name=f"inproj_l{layer}",
    )(x, mod, mod, norm_g, w_in)


def _conv_kernel(prev_ref, cur_ref, next_ref, w_ref, b_ref, lg_ref, lb_ref, o_ref, hs_ref, cv_ref,
                 *, tp, dc, prompt_tiles, tiles_per_seq):
    i = pl.program_id(0)

    def glu(ref):
        z = ref[...].astype(F32)
        return z[:, :dc] * jax.nn.sigmoid(z[:, dc:])

    pos = lax.rem(i - prompt_tiles, tiles_per_seq)
    is_latent = i >= prompt_tiles
    has_prev = jnp.logical_and(is_latent, pos != 0)
    has_next = jnp.logical_and(is_latent, pos != tiles_per_seq - 1)
    hs_ref[0:HALO, :] = jnp.where(has_prev, glu(prev_ref), 0.0)
    hs_ref[HALO:HALO + tp, :] = glu(cur_ref)
    hs_ref[HALO + tp:2 * HALO + tp, :] = jnp.where(has_next, glu(next_ref), 0.0)

    pad = CONV_WIDTH // 2
    rc = 32
    for c0 in range(0, dc, 128):
        for r0 in range(0, tp, rc):
            acc = jnp.zeros((rc, 128), F32)
            for k in range(CONV_WIDTH):
                start = HALO - pad + r0 + k
                acc = acc + hs_ref[start:start + rc, c0:c0 + 128] * w_ref[k:k + 1, c0:c0 + 128]
            cv_ref[r0:r0 + rc, c0:c0 + 128] = acc

    h = cv_ref[...] + b_ref[...]
    mu = jnp.mean(h, axis=-1, keepdims=True)
    hc = h - mu
    var = jnp.mean(hc * hc, axis=-1, keepdims=True)
    y = hc * lax.rsqrt(var + EPS) * lg_ref[...] + lb_ref[...]
    o_ref[...] = _silu(y).astype(o_ref.dtype)


def _conv_branch(a, conv_w, conv_b, ln_g, ln_b, layer, n_prompt, seq, dec_seq, tp=256):
    m_tok = a.shape[0]
    dc = conv_w.shape[-1]
    assert seq == tp and dec_seq % tp == 0
    hb = tp // HALO
    n_hblk = m_tok // HALO
    kern = functools.partial(_conv_kernel, tp=tp, dc=dc, prompt_tiles=n_prompt // tp,
                             tiles_per_seq=dec_seq // tp)
    vec = pl.BlockSpec((None, 1, dc), lambda i: (layer, 0, 0))
    return pl.pallas_call(
        kern,
        grid=(m_tok // tp,),
        in_specs=[
            pl.BlockSpec((HALO, 2 * dc), lambda i: (jnp.maximum(i * hb - 1, 0), 0)),
            pl.BlockSpec((tp, 2 * dc), lambda i: (i, 0)),
            pl.BlockSpec((HALO, 2 * dc), lambda i: (jnp.minimum((i + 1) * hb, n_hblk - 1), 0)),
            pl.BlockSpec((None, CONV_WIDTH, dc), lambda i: (layer, 0, 0)),
            vec, vec, vec,
        ],
        out_specs=pl.BlockSpec((tp, dc), lambda i: (i, 0)),
        out_shape=jax.ShapeDtypeStruct((m_tok, dc), BF16),
        scratch_shapes=[pltpu.VMEM((tp + 2 * HALO, dc), F32), pltpu.VMEM((tp, dc), F32)],
        compiler_params=_params(("parallel",)),
        name=f"conv_l{layer}",
    )(a, a, a, conv_w, conv_b.reshape(-1, 1, dc), ln_g.reshape(-1, 1, dc), ln_b.reshape(-1, 1, dc))


def _dot_nt(a, b):
    return lax.dot_general(a, b, (((1,), (1,)), ((), ())), preferred_element_type=F32)


def _ctx_attn_kernel(q_ref, k_ref, v_ref, o_ref):
    scale = HEAD_DIM ** -0.5
    outs = []
    for h in range(N_HEADS):
        sl = slice(h * HEAD_DIM, (h + 1) * HEAD_DIM)
        q = q_ref[:, sl]
        k = k_ref[:, sl].astype(BF16)
        v = v_ref[:, sl].astype(BF16)
        s = _dot_nt(q, k) * scale
        e = jnp.exp(s - jnp.max(s, axis=-1, keepdims=True))
        p = e / jnp.sum(e, axis=-1, keepdims=True)
        outs.append(jnp.dot(p.astype(BF16), v, preferred_element_type=F32))
    o_ref[...] = jnp.concatenate(outs, axis=-1).astype(o_ref.dtype)


def _ctx_attention(a, kv, batch, seq, q_col0):
    d_na = N_HEADS * HEAD_DIM
    qb = q_col0 // d_na
    return pl.pallas_call(
        _ctx_attn_kernel,
        grid=(batch,),
        in_specs=[
            pl.BlockSpec((seq, d_na), lambda b: (b, qb)),
            pl.BlockSpec((seq, d_na), lambda b: (b, 0)),
            pl.BlockSpec((seq, d_na), lambda b: (b, 1)),
        ],
        out_specs=pl.BlockSpec((seq, d_na), lambda b: (b, 0)),
        out_shape=jax.ShapeDtypeStruct((batch * seq, d_na), BF16),
        compiler_params=_params(("parallel",)),
        name="ctx_attn",
    )(a, kv, kv)


NA_QROWS = 4
NA_KROWS = NA_QROWS + WIN_ROWS


def _na_kernel(q_ref, k_ref, v_ref, ck_ref, cv_ref, bias_ref, o_ref, *, mb, rows):
    n_blk = rows // NA_QROWS
    qn = NA_QROWS * GRID_W
    kn = NA_KROWS * GRID_W
    first = lax.broadcasted_iota(jnp.int32, (qn, 2 * HEAD_DIM), 1) < HEAD_DIM
    ck = ck_ref[...]
    cv = cv_ref[...]
    for mi in range(mb):
        m = pl.program_id(2) * mb + mi
        kr0 = jnp.clip(NA_QROWS * m - WIN_ROWS // 2, 0, rows - NA_KROWS)
        k0 = pl.multiple_of(kr0 * GRID_W, NA_QROWS * GRID_W)
        ku = k_ref[pl.ds(k0, kn), :].astype(BF16)
        vu = v_ref[pl.ds(k0, kn), :].astype(BF16)
        variant = jnp.where(m == 0, 0, jnp.where(m == n_blk - 1, 2, 1))
        q = q_ref[mi * qn:(mi + 1) * qn, :].astype(F32) * (HEAD_DIM ** -0.5)
        out = None
        for hh in range(2):
            own = first if hh == 0 else jnp.logical_not(first)
            qh = jnp.where(own, q, 0.0).astype(BF16)
            s_loc = _dot_nt(qh, ku) + bias_ref[variant, hh]
            s_ctx = _dot_nt(qh, ck)
            mx = jnp.maximum(jnp.max(s_loc, axis=-1, keepdims=True), jnp.max(s_ctx, axis=-1, keepdims=True))
            e_loc = jnp.exp(s_loc - mx)
            e_ctx = jnp.exp(s_ctx - mx)
            denom = jnp.sum(e_loc, axis=-1, keepdims=True) + jnp.sum(e_ctx, axis=-1, keepdims=True)
            o = (jnp.dot(e_loc.astype(BF16), vu, preferred_element_type=F32)
                 + jnp.dot(e_ctx.astype(BF16), cv, preferred_element_type=F32)) / denom
            out = o if hh == 0 else jnp.where(first, out, o)
        o_ref[mi * qn:(mi + 1) * qn, :] = out.astype(o_ref.dtype)


def _na_bias_table(rel_bias, rows):
    n_blk = rows // NA_QROWS
    assert rows % NA_QROWS == 0 and n_blk >= 3 and rows >= NA_KROWS + WIN_ROWS // 2
    w = jnp.arange(GRID_W)[:, None]
    j = jnp.arange(GRID_W)[None, :]
    cs = jnp.clip(w - WIN_COLS // 2, 0, GRID_W - WIN_COLS)
    vcol = (j >= cs) & (j < cs + WIN_COLS)
    dcol = jnp.clip(j - w + (WIN_COLS - 1), 0, 2 * WIN_COLS - 2)
    tabs = []
    for m in (0, 1, n_blk - 1):
        kr0 = min(max(NA_QROWS * m - WIN_ROWS // 2, 0), rows - NA_KROWS)
        r = NA_QROWS * m + jnp.arange(NA_QROWS)[:, None]
        kr = kr0 + jnp.arange(NA_KROWS)[None, :]
        rs = jnp.clip(r - WIN_ROWS // 2, 0, rows - WIN_ROWS)
        vrow = (kr >= rs) & (kr < rs + WIN_ROWS)
        drow = jnp.clip(kr - r + (WIN_ROWS - 1), 0, 2 * WIN_ROWS - 2)
        b = rel_bias[:, drow[:, :, None, None], dcol[None, None, :, :]]
        b = jnp.where((vrow[:, :, None, None] & vcol[None, None])[None], b, NEG_INF)
        b = jnp.transpose(b, (0, 1, 3, 2, 4))
        tabs.append(b.reshape(rel_bias.shape[0], NA_QROWS * GRID_W, NA_KROWS * GRID_W))
    return jnp.stack(tabs, axis=0)


def _na_attention(a, kv, ck, cv, bias, layer, n_prompt, dec_batch, dec_seq, q_col0, mb=4):
    d_na = N_HEADS * HEAD_DIM
    rows = dec_seq // GRID_W
    hpw = 2 * HEAD_DIM
    qrows = mb * NA_QROWS * GRID_W
    steps = dec_seq // qrows
    past = ck.shape[2]
    assert n_prompt % qrows == 0 and n_prompt % dec_seq == 0 and dec_seq % qrows == 0
    kern = functools.partial(_na_kernel, mb=mb, rows=rows)
    q_blk0 = n_prompt // qrows
    kv_blk0 = n_prompt // dec_seq
    return pl.pallas_call(
        kern,
        grid=(N_HEADS // 2, dec_batch, steps),
        in_specs=[
            pl.BlockSpec((qrows, hpw), lambda hp, b, r: (q_blk0 + b * steps + r, q_col0 // hpw + hp)),
            pl.BlockSpec((dec_seq, hpw), lambda hp, b, r: (kv_blk0 + b, hp)),
            pl.BlockSpec((dec_seq, hpw), lambda hp, b, r: (kv_blk0 + b, d_na // hpw + hp)),
            pl.BlockSpec((None, None, past, hpw), lambda hp, b, r: (b, layer, 0, hp)),
            pl.BlockSpec((None, None, past, hpw), lambda hp, b, r: (b, layer, 0, hp)),
            pl.BlockSpec((3, 2, NA_QROWS * GRID_W, NA_KROWS * GRID_W), lambda hp, b, r: (0, hp, 0, 0)),
        ],
        out_specs=pl.BlockSpec((qrows, hpw), lambda hp, b, r: (b * steps + r, hp)),
        out_shape=jax.ShapeDtypeStruct((dec_batch * dec_seq, d_na), BF16),
        compiler_params=_params(("parallel", "parallel", "arbitrary")),
        name=f"na_l{layer}",
    )(a, kv, kv, ck, cv, bias)


def _merge_kernel(x_ref, gate_ref, ch_ref, no_ref, gc_ref, gn_ref, wc_ref, wn_ref, wo_ref, o_ref):
    yc = jnp.dot(ch_ref[...], wc_ref[...], preferred_element_type=F32)
    yn = jnp.dot(no_ref[...], wn_ref[...], preferred_element_type=F32)
    m = (jax.nn.sigmoid(gc_ref[...].astype(F32)) * yc + jax.nn.sigmoid(gn_ref[...].astype(F32)) * yn)
    o_ref[...] = x_ref[...] + gate_ref[...] * jnp.dot(m.astype(BF16), wo_ref[...], preferred_element_type=F32)


def _merge(x, mod, conv_h, na_o, gates, w_conv_out, w_na_out, w_out, layer, n_prompt, dec_seq, tm=256):
    m_tok, d = x.shape
    dc = conv_h.shape[1]
    dn = na_o.shape[1]
    mspec = functools.partial(_mod_spec, layer, tm=tm, d=d, n_prompt=n_prompt, dec_seq=dec_seq)
    return pl.pallas_call(
        _merge_kernel,
        grid=(m_tok // tm,),
        in_specs=[
            pl.BlockSpec((tm, d), lambda i: (i, 0)),
            mspec(m=5),
            pl.BlockSpec((tm, dc), lambda i: (i, 0)),
            pl.BlockSpec((tm, dn), lambda i: (i, 0)),
            pl.BlockSpec((tm, d), lambda i: (i, 0)),
            pl.BlockSpec((tm, d), lambda i: (i, 1)),
            pl.BlockSpec((None, dc, d), lambda i: (layer, 0, 0)),
            pl.BlockSpec((None, dn, d), lambda i: (layer, 0, 0)),
            pl.BlockSpec((None, d, d), lambda i: (layer, 0, 0)),
        ],
        out_specs=pl.BlockSpec((tm, d), lambda i: (i, 0)),
        out_shape=jax.ShapeDtypeStruct((m_tok, d), F32),
        compiler_params=_params(("parallel",)),
        name=f"merge_l{layer}",
    )(x, mod, conv_h, na_o, gates, gates, w_conv_out, w_na_out, w_out)


def _final_norm_kernel(x_ref, g_ref, o_ref):
    x = x_ref[...]
    ms = jnp.mean(x * x, axis=-1, keepdims=True)
    o_ref[...] = (x * lax.rsqrt(ms + EPS)) * g_ref[...]


def _final_norm(x, g, row0, nrows, tm=512):
    d = x.shape[1]
    b0 = row0 // tm
    return pl.pallas_call(
        _final_norm_kernel,
        grid=(nrows // tm,),
        in_specs=[pl.BlockSpec((tm, d), lambda i: (b0 + i, 0)), pl.BlockSpec((1, d), lambda i: (0, 0))],
        out_specs=pl.BlockSpec((tm, d), lambda i: (i, 0)),
        out_shape=jax.ShapeDtypeStruct((nrows, d), F32),
        compiler_params=_params(("parallel",)),
        name=f"final_norm_r{row0}",
    )(x, g.reshape(1, d))


def kernel(x_prompt, x_sample, cache_k, cache_v, c, c_ctx, w_ada, b_ada, norm_g, ffn_w_gate, ffn_w_up,
           ffn_w_down, w_in, conv_w, conv_b, conv_ln_g, conv_ln_b, w_conv_out, rel_bias, w_na_out, w_out,
           final_g):
    batch, seq, d = x_prompt.shape
    dec_batch, dec_seq, _ = x_sample.shape
    depth = w_ada.shape[0]
    d_ff = ffn_w_gate.shape[-1]
    d_na = N_HEADS * HEAD_DIM
    dc = conv_w.shape[-1]
    past = cache_k.shape[2]
    n_prompt = batch * seq
    n_latent = dec_batch * dec_seq

    x = jnp.concatenate([x_prompt.reshape(n_prompt, d), x_sample.reshape(n_latent, d)], axis=0)

    cvec = jnp.concatenate([c_ctx[None, :], c, jnp.zeros((MOD_ROWS - 1 - dec_batch, d), F32)], axis=0)
    mod = _ada_table(cvec, w_ada, b_ada).reshape(depth, MOD_ROWS, N_MOD, 1, d)

    ff_pad = (-d_ff) % FFN_CHUNK
    n_ff = (d_ff + ff_pad) // FFN_CHUNK

    def ffn_cols(w):
        w = jnp.pad(w.astype(BF16), ((0, 0), (0, 0), (0, 0), (0, ff_pad)))
        return jnp.transpose(w.reshape(depth, 2, d, n_ff, FFN_CHUNK), (0, 1, 3, 2, 4))

    wg = ffn_cols(ffn_w_gate)
    wu = ffn_cols(ffn_w_up)
    wd = jnp.pad(ffn_w_down.astype(BF16), ((0, 0), (0, 0), (0, ff_pad), (0, 0)))
    in_width = w_in.shape[-1]
    w_in_b = jnp.transpose(w_in.astype(BF16).reshape(depth, d, in_width // INPROJ_CHUNK, INPROJ_CHUNK),
                           (0, 2, 1, 3))
    w_co_b = w_conv_out.astype(BF16)
    w_no_b = w_na_out.astype(BF16)
    w_o_b = w_out.astype(BF16)
    ck = cache_k.reshape(dec_batch, depth, past, d_na).astype(BF16)
    cv = cache_v.reshape(dec_batch, depth, past, d_na).astype(BF16)
    norm_g4 = norm_g.reshape(depth, 3, 1, d)

    a_cols = 2 * dc + d_na
    rows = dec_seq // GRID_W

    keys, vals = [], []
    for l in range(depth):
        x = _ffn(x, mod, norm_g4, wg, wu, wd, l, 0, 0, n_prompt, dec_seq)

        a, kv, gates = _inproj(x, mod, norm_g4, w_in_b, l, a_cols, 2 * d_na, 2 * d, n_prompt, dec_seq)
        keys.append(kv[:n_prompt, :d_na].reshape(batch, seq, N_HEADS, HEAD_DIM))
        vals.append(kv[:n_prompt, d_na:].reshape(batch, seq, N_HEADS, HEAD_DIM))

        conv_h = _conv_branch(a, conv_w, conv_b, conv_ln_g, conv_ln_b, l, n_prompt, seq, dec_seq)
        na_p = _ctx_attention(a, kv, batch, seq, 2 * dc)
        na_s = _na_attention(a, kv, ck, cv, _na_bias_table(rel_bias[l], rows), l, n_prompt, dec_batch,
                             dec_seq, 2 * dc)
        na_o = jnp.concatenate([na_p, na_s], axis=0)
        x = _merge(x, mod, conv_h, na_o, gates, w_co_b, w_no_b, w_o_b, l, n_prompt, dec_seq)

        x = _ffn(x, mod, norm_g4, wg, wu, wd, l, 1, 6, n_prompt, dec_seq)

    y_prompt = _final_norm(x, final_g, 0, n_prompt).reshape(batch, seq, d)
    y_sample = _final_norm(x, final_g, n_prompt, n_latent).reshape(dec_batch, dec_seq, d)
    return (y_prompt, y_sample, jnp.stack(keys, axis=1), jnp.stack(vals, axis=1))
```

```python
import functools

import jax
import jax.numpy as jnp
from jax import lax
from jax.experimental import pallas as pl
from jax.experimental.pallas import tpu as pltpu

F32 = jnp.float32
BF16 = jnp.bfloat16

N_HEADS = 16
HEAD_DIM = 64
GRID_W = 64
WIN_ROWS = 8
WIN_COLS = 16
CONV_WIDTH = 31
N_MOD = 9
FFN_RES = 0.5
EPS = 1e-6
NEG_INF = -1e30

SUBLANES = 8
LANES = 128
FFN_CHUNK = 512
INPROJ_CHUNK = 512
MOD_ROWS = 8
HALO = 16
VMEM_LIMIT = 56 * 1024 * 1024


def _params(sem, vmem=VMEM_LIMIT):
    return pltpu.CompilerParams(dimension_semantics=sem, vmem_limit_bytes=vmem)


def _silu(x):
    return x * jax.nn.sigmoid(x)


def _ada_kernel(c_ref, w_ref, b_ref, o_ref):
    s = _silu(c_ref[...]).astype(BF16)
    o_ref[...] = jnp.dot(s, w_ref[...].astype(BF16), preferred_element_type=F32) + b_ref[...]


def _ada_table(cvec, w_ada, b_ada):
    depth, d, n = w_ada.shape
    tn = min(1024, d)
    assert n % tn == 0
    return pl.pallas_call(
        _ada_kernel,
        grid=(depth, n // tn),
        in_specs=[
            pl.BlockSpec((MOD_ROWS, d), lambda l, j: (0, 0)),
            pl.BlockSpec((None, d, tn), lambda l, j: (l, 0, j)),
            pl.BlockSpec((None, 1, tn), lambda l, j: (l, 0, j)),
        ],
        out_specs=pl.BlockSpec((None, MOD_ROWS, tn), lambda l, j: (l, 0, j)),
        out_shape=jax.ShapeDtypeStruct((depth, MOD_ROWS, n), F32),
        compiler_params=_params(("arbitrary", "arbitrary")),
        name="ada_table",
    )(cvec, w_ada, b_ada.reshape(depth, 1, n))


def _mod_spec(layer, m, tm, d, n_prompt, dec_seq):
    def idx(i, *_):
        row0 = i * tm
        grp = jnp.where(row0 < n_prompt, 0, 1 + (row0 - n_prompt) // dec_seq)
        return (layer, grp, m, 0, 0)
    return pl.BlockSpec((None, None, None, 1, d), idx)


NORM_ROWS = 16


def _store_norm_mod(x_ref, g_ref, shift_ref, scale_ref, hn_ref):
    gain = g_ref[...] * (1.0 + scale_ref[...])
    shift = shift_ref[...]

    def body(i, carry):
        r = pl.multiple_of(i * NORM_ROWS, NORM_ROWS)
        x = x_ref[pl.ds(r, NORM_ROWS), :]
        ms = jnp.mean(x * x, axis=-1, keepdims=True)
        hn_ref[pl.ds(r, NORM_ROWS), :] = (x * lax.rsqrt(ms + EPS) * gain + shift).astype(hn_ref.dtype)
        return carry

    lax.fori_loop(0, x_ref.shape[0] // NORM_ROWS, body, 0, unroll=8)


def _ffn_kernel(x_ref, shift_ref, scale_ref, gate_ref, g_ref, wg_ref, wu_ref, wd_ref, o_ref, hn_ref,
                *, n_split):
    j = pl.program_id(1)

    @pl.when(j == 0)
    def _():
        _store_norm_mod(x_ref, g_ref, shift_ref, scale_ref, hn_ref)
        o_ref[...] = jnp.zeros_like(o_ref)

    hn = hn_ref[...]
    cw = wg_ref.shape[-1] // n_split
    part = None
    for c in range(n_split):
        sl = slice(c * cw, (c + 1) * cw)
        a = jnp.dot(hn, wg_ref[:, sl], preferred_element_type=F32)
        u = jnp.dot(hn, wu_ref[:, sl], preferred_element_type=F32)
        h = (_silu(a) * u).astype(BF16)
        p = jnp.dot(h, wd_ref[sl, :], preferred_element_type=F32)
        part = p if part is None else part + p
    o_ref[...] += part

    @pl.when(j == pl.num_programs(1) - 1)
    def _():
        o_ref[...] = x_ref[...] + FFN_RES * gate_ref[...] * o_ref[...]


def _ffn(x, mod, norm_g, wg, wu, wd, layer, sub, mod0, n_prompt, dec_seq, tm=512, tf=FFN_CHUNK, n_split=2):
    m_tok, d = x.shape
    ffp = wg.shape[-1]
    mspec = functools.partial(_mod_spec, layer, tm=tm, d=d, n_prompt=n_prompt, dec_seq=dec_seq)
    return pl.pallas_call(
        functools.partial(_ffn_kernel, n_split=n_split),
        grid=(m_tok // tm, ffp // tf),
        in_specs=[
            pl.BlockSpec((tm, d), lambda i, j: (i, 0)),
            mspec(m=mod0), mspec(m=mod0 + 1), mspec(m=mod0 + 2),
            pl.BlockSpec((None, None, 1, d), lambda i, j: (layer, 2 * sub, 0, 0)),
            pl.BlockSpec((None, None, d, tf), lambda i, j: (layer, sub, 0, j)),
            pl.BlockSpec((None, None, d, tf), lambda i, j: (layer, sub, 0, j)),
            pl.BlockSpec((None, None, tf, d), lambda i, j: (layer, sub, j, 0)),
        ],
        out_specs=pl.BlockSpec((tm, d), lambda i, j: (i, 0)),
        out_shape=jax.ShapeDtypeStruct((m_tok, d), F32),
        scratch_shapes=[pltpu.VMEM((tm, d), BF16)],
        compiler_params=_params(("parallel", "arbitrary")),
        name=f"ffn_l{layer}_s{sub}",
    )(x, mod, mod, mod, norm_g, wg, wu, wd)


def _inproj_kernel(x_ref, shift_ref, scale_ref, g_ref, w_ref, a_ref, kv_ref, gt_ref, hn_ref,
                   *, na, nkv, n_split):
    j = pl.program_id(1)

    @pl.when(j == 0)
    def _():
        _store_norm_mod(x_ref, g_ref, shift_ref, scale_ref, hn_ref)

    cw = w_ref.shape[-1] // n_split

    def project(o_ref):
        hn = hn_ref[...]
        for c in range(n_split):
            cs = slice(c * cw, (c + 1) * cw)
            o_ref[:, cs] = jnp.dot(hn, w_ref[:, cs], preferred_element_type=F32).astype(o_ref.dtype)

    @pl.when(j < na)
    def _():
        project(a_ref)

    @pl.when(jnp.logical_and(j >= na, j < na + nkv))
    def _():
        project(kv_ref)

    @pl.when(j >= na + nkv)
    def _():
        project(gt_ref)


def _inproj(x, mod, norm_g, w_in, layer, a_cols, kv_cols, g_cols, n_prompt, dec_seq, tm=1024, tn=INPROJ_CHUNK,
            n_split=2):
    m_tok, d = x.shape
    n_chunks = w_in.shape[-1] // tn
    na, nkv, ng = a_cols // tn, kv_cols // tn, g_cols // tn
    assert na * tn == a_cols and nkv * tn == kv_cols and ng * tn == g_cols and na + nkv + ng == n_chunks
    mspec = functools.partial(_mod_spec, layer, tm=tm, d=d, n_prompt=n_prompt, dec_seq=dec_seq)
    return pl.pallas_call(
        functools.partial(_inproj_kernel, na=na, nkv=nkv, n_split=n_split),
        grid=(m_tok // tm, n_chunks),
        in_specs=[
            pl.BlockSpec((tm, d), lambda i, j: (i, 0)),
            mspec(m=3), mspec(m=4),
            pl.BlockSpec((None, None, 1, d), lambda i, j: (layer, 1, 0, 0)),
            pl.BlockSpec((None, d, tn), lambda i, j: (layer, 0, j)),
        ],
        out_specs=[
            pl.BlockSpec((tm, tn), lambda i, j: (i, jnp.minimum(j, na - 1))),
            pl.BlockSpec((tm, tn), lambda i, j: (i, jnp.clip(j - na, 0, nkv - 1))),
            pl.BlockSpec((tm, tn), lambda i, j: (i, jnp.clip(j - na - nkv, 0, ng - 1))),
        ],
        out_shape=[
            jax.ShapeDtypeStruct((m_tok, a_cols), BF16),
            jax.ShapeDtypeStruct((m_tok, kv_cols), F32),
            jax.ShapeDtypeStruct((m_tok, g_cols), BF16),
        ],
        scratch_shapes=[pltpu.VMEM((tm, d), BF16)],
        compiler_params=_params(("parallel", "arbitrary")),
        name=f"inproj_l{layer}",
    )(x, mod, mod, norm_g, w_in)


def _conv_kernel(prev_ref, cur_ref, next_ref, w_ref, b_ref, lg_ref, lb_ref, o_ref, hs_ref, sh_ref, cv_ref,
                 *, tp, dc, prompt_tiles, tiles_per_seq):
    i = pl.program_id(0)

    def glu(ref):
        z = ref[...].astype(F32)
        return z[:, :dc] * jax.nn.sigmoid(z[:, dc:])

    pos = lax.rem(i - prompt_tiles, tiles_per_seq)
    is_latent = i >= prompt_tiles
    has_prev = jnp.logical_and(is_latent, pos != 0)
    has_next = jnp.logical_and(is_latent, pos != tiles_per_seq - 1)
    hs_ref[0:HALO, :] = jnp.where(has_prev, glu(prev_ref), 0.0)
    hs_ref[HALO:HALO + tp, :] = glu(cur_ref)
    hs_ref[HALO + tp:2 * HALO + tp, :] = jnp.where(has_next, glu(next_ref), 0.0)

    base = HALO - CONV_WIDTH // 2
    ext = sh_ref.shape[1]
    for s in range(SUBLANES):
        sh_ref[s] = hs_ref[base + s:base + s + ext, :]

    rc = 32
    for c0 in range(0, dc, LANES):
        for r0 in range(0, tp, rc):
            acc = jnp.zeros((rc, LANES), F32)
            for k in range(CONV_WIDTH):
                st = r0 + SUBLANES * (k // SUBLANES)
                acc = acc + sh_ref[k % SUBLANES, st:st + rc, c0:c0 + LANES] * w_ref[k:k + 1, c0:c0 + LANES]
            cv_ref[r0:r0 + rc, c0:c0 + LANES] = acc

    h = cv_ref[...] + b_ref[...]
    mu = jnp.mean(h, axis=-1, keepdims=True)
    hc = h - mu
    var = jnp.mean(hc * hc, axis=-1, keepdims=True)
    y = hc * lax.rsqrt(var + EPS) * lg_ref[...] + lb_ref[...]
    o_ref[...] = _silu(y).astype(o_ref.dtype)


def _conv_branch(a, conv_w, conv_b, ln_g, ln_b, layer, n_prompt, seq, dec_seq, tp=256):
    m_tok = a.shape[0]
    dc = conv_w.shape[-1]
    assert seq == tp and dec_seq % tp == 0
    hb = tp // HALO
    n_hblk = m_tok // HALO
    kern = functools.partial(_conv_kernel, tp=tp, dc=dc, prompt_tiles=n_prompt // tp,
                             tiles_per_seq=dec_seq // tp)
    vec = pl.BlockSpec((None, 1, dc), lambda i: (layer, 0, 0))
    return pl.pallas_call(
        kern,
        grid=(m_tok // tp,),
        in_specs=[
            pl.BlockSpec((HALO, 2 * dc), lambda i: (jnp.maximum(i * hb - 1, 0), 0)),
            pl.BlockSpec((tp, 2 * dc), lambda i: (i, 0)),
            pl.BlockSpec((HALO, 2 * dc), lambda i: (jnp.minimum((i + 1) * hb, n_hblk - 1), 0)),
            pl.BlockSpec((None, CONV_WIDTH, dc), lambda i: (layer, 0, 0)),
            vec, vec, vec,
        ],
        out_specs=pl.BlockSpec((tp, dc), lambda i: (i, 0)),
        out_shape=jax.ShapeDtypeStruct((m_tok, dc), BF16),
        scratch_shapes=[
            pltpu.VMEM((tp + 2 * HALO, dc), F32),
            pltpu.VMEM((SUBLANES, tp + SUBLANES * ((CONV_WIDTH - 1) // SUBLANES), dc), F32),
            pltpu.VMEM((tp, dc), F32),
        ],
        compiler_params=_params(("parallel",)),
        name=f"conv_l{layer}",
    )(a, a, a, conv_w, conv_b.reshape(-1, 1, dc), ln_g.reshape(-1, 1, dc), ln_b.reshape(-1, 1, dc))


def _dot_nt(a, b):
    return lax.dot_general(a, b, (((1,), (1,)), ((), ())), preferred_element_type=F32)


def _ctx_attn_kernel(q_ref, k_ref, v_ref, o_ref):
    scale = HEAD_DIM ** -0.5
    outs = []
    for h in range(N_HEADS):
        sl = slice(h * HEAD_DIM, (h + 1) * HEAD_DIM)
        q = q_ref[:, sl]
        k = k_ref[:, sl].astype(BF16)
        v = v_ref[:, sl].astype(BF16)
        s = _dot_nt(q, k) * scale
        e = jnp.exp(s - jnp.max(s, axis=-1, keepdims=True))
        p = e / jnp.sum(e, axis=-1, keepdims=True)
        outs.append(jnp.dot(p.astype(BF16), v, preferred_element_type=F32))
    o_ref[...] = jnp.concatenate(outs, axis=-1).astype(o_ref.dtype)


def _ctx_attention(a, kv, batch, seq, q_col0):
    d_na = N_HEADS * HEAD_DIM
    qb = q_col0 // d_na
    return pl.pallas_call(
        _ctx_attn_kernel,
        grid=(batch,),
        in_specs=[
            pl.BlockSpec((seq, d_na), lambda b: (b, qb)),
            pl.BlockSpec((seq, d_na), lambda b: (b, 0)),
            pl.BlockSpec((seq, d_na), lambda b: (b, 1)),
        ],
        out_specs=pl.BlockSpec((seq, d_na), lambda b: (b, 0)),
        out_shape=jax.ShapeDtypeStruct((batch * seq, d_na), BF16),
        compiler_params=_params(("parallel",)),
        name="ctx_attn",
    )(a, kv, kv)


NA_QROWS = 4
NA_KROWS = NA_QROWS + WIN_ROWS


def _na_kernel(q_ref, k_ref, v_ref, ck_ref, cv_ref, bias_ref, o_ref, *, mb, rows):
    n_blk = rows // NA_QROWS
    qn = NA_QROWS * GRID_W
    kn = NA_KROWS * GRID_W
    first = lax.broadcasted_iota(jnp.int32, (qn, 2 * HEAD_DIM), 1) < HEAD_DIM
    ck = ck_ref[...]
    cv = cv_ref[...]
    for mi in range(mb):
        m = pl.program_id(2) * mb + mi
        kr0 = jnp.clip(NA_QROWS * m - WIN_ROWS // 2, 0, rows - NA_KROWS)
        k0 = pl.multiple_of(kr0 * GRID_W, NA_QROWS * GRID_W)
        ku = k_ref[pl.ds(k0, kn), :].astype(BF16)
        vu = v_ref[pl.ds(k0, kn), :].astype(BF16)
        variant = jnp.where(m == 0, 0, jnp.where(m == n_blk - 1, 2, 1))
        q = q_ref[mi * qn:(mi + 1) * qn, :].astype(F32) * (HEAD_DIM ** -0.5)
        out = None
        for hh in range(2):
            own = first if hh == 0 else jnp.logical_not(first)
            qh = jnp.where(own, q, 0.0).astype(BF16)
            s_loc = _dot_nt(qh, ku) + bias_ref[variant, hh]
            s_ctx = _dot_nt(qh, ck)
            mx = jnp.maximum(jnp.max(s_loc, axis=-1, keepdims=True), jnp.max(s_ctx, axis=-1, keepdims=True))
            e_loc = jnp.exp(s_loc - mx)
            e_ctx = jnp.exp(s_ctx - mx)
            denom = jnp.sum(e_loc, axis=-1, keepdims=True) + jnp.sum(e_ctx, axis=-1, keepdims=True)
            o = (jnp.dot(e_loc.astype(BF16), vu, preferred_element_type=F32)
                 + jnp.dot(e_ctx.astype(BF16), cv, preferred_element_type=F32)) / denom
            out = o if hh == 0 else jnp.where(first, out, o)
        o_ref[mi * qn:(mi + 1) * qn, :] = out.astype(o_ref.dtype)


def _na_bias_table(rel_bias, rows):
    n_blk = rows // NA_QROWS
    assert rows % NA_QROWS == 0 and n_blk >= 3 and rows >= NA_KROWS + WIN_ROWS // 2
    w = jnp.arange(GRID_W)[:, None]
    j = jnp.arange(GRID_W)[None, :]
    cs = jnp.clip(w - WIN_COLS // 2, 0, GRID_W - WIN_COLS)
    vcol = (j >= cs) & (j < cs + WIN_COLS)
    dcol = jnp.clip(j - w + (WIN_COLS - 1), 0, 2 * WIN_COLS - 2)
    n_heads = rel_bias.shape[0]
    cols = jnp.where(vcol[None, :, None, :], jnp.transpose(rel_bias[:, :, dcol], (0, 2, 1, 3)), NEG_INF)

    def masked(n):
        return jnp.full((n_heads, GRID_W, n, GRID_W), NEG_INF, F32)

    tabs = []
    for m in (0, 1, n_blk - 1):
        kr0 = min(max(NA_QROWS * m - WIN_ROWS // 2, 0), rows - NA_KROWS)
        blocks = []
        for rq in range(NA_QROWS):
            r = NA_QROWS * m + rq
            rs = min(max(r - WIN_ROWS // 2, 0), rows - WIN_ROWS)
            u0 = rs - kr0
            d0 = rs - r + (WIN_ROWS - 1)
            parts = [masked(u0), cols[:, :, d0:d0 + WIN_ROWS, :], masked(NA_KROWS - WIN_ROWS - u0)]
            blk = jnp.concatenate([p for p in parts if p.shape[2]], axis=2)
            blocks.append(blk.reshape(n_heads, GRID_W, NA_KROWS * GRID_W))
        tabs.append(jnp.concatenate(blocks, axis=1))
    return jnp.stack(tabs, axis=0)


def _na_attention(a, kv, ck, cv, bias, layer, n_prompt, dec_batch, dec_seq, q_col0, mb=4):
    d_na = N_HEADS * HEAD_DIM
    rows = dec_seq // GRID_W
    hpw = 2 * HEAD_DIM
    qrows = mb * NA_QROWS * GRID_W
    steps = dec_seq // qrows
    past = ck.shape[2]
    assert n_prompt % qrows == 0 and n_prompt % dec_seq == 0 and dec_seq % qrows == 0
    kern = functools.partial(_na_kernel, mb=mb, rows=rows)
    q_blk0 = n_prompt // qrows
    kv_blk0 = n_prompt // dec_seq
    return pl.pallas_call(
        kern,
        grid=(N_HEADS // 2, dec_batch, steps),
        in_specs=[
            pl.BlockSpec((qrows, hpw), lambda hp, b, r: (q_blk0 + b * steps + r, q_col0 // hpw + hp)),
            pl.BlockSpec((dec_seq, hpw), lambda hp, b, r: (kv_blk0 + b, hp)),
            pl.BlockSpec((dec_seq, hpw), lambda hp, b, r: (kv_blk0 + b, d_na // hpw + hp)),
            pl.BlockSpec((None, None, past, hpw), lambda hp, b, r: (b, layer, 0, hp)),
            pl.BlockSpec((None, None, past, hpw), lambda hp, b, r: (b, layer, 0, hp)),
            pl.BlockSpec((3, 2, NA_QROWS * GRID_W, NA_KROWS * GRID_W), lambda hp, b, r: (0, hp, 0, 0)),
        ],
        out_specs=pl.BlockSpec((qrows, hpw), lambda hp, b, r: (b * steps + r, hp)),
        out_shape=jax.ShapeDtypeStruct((dec_batch * dec_seq, d_na), BF16),
        compiler_params=_params(("parallel", "parallel", "arbitrary")),
        name=f"na_l{layer}",
    )(a, kv, kv, ck, cv, bias)


def _merge_kernel(x_ref, gate_ref, ch_ref, nop_ref, nos_ref, gc_ref, gn_ref, wc_ref, wn_ref, wo_ref, o_ref,
                  *, prompt_tiles):
    no = jnp.where(pl.program_id(0) < prompt_tiles, nop_ref[...], nos_ref[...])
    yc = jnp.dot(ch_ref[...], wc_ref[...], preferred_element_type=F32)
    yn = jnp.dot(no, wn_ref[...], preferred_element_type=F32)
    m = (jax.nn.sigmoid(gc_ref[...].astype(F32)) * yc + jax.nn.sigmoid(gn_ref[...].astype(F32)) * yn)
    o_ref[...] = x_ref[...] + gate_ref[...] * jnp.dot(m.astype(BF16), wo_ref[...], preferred_element_type=F32)


def _merge(x, mod, conv_h, na_p, na_s, gates, w_conv_out, w_na_out, w_out, layer, n_prompt, dec_seq, tm=256):
    m_tok, d = x.shape
    dc = conv_h.shape[1]
    dn = na_p.shape[1]
    pt = n_prompt // tm
    mspec = functools.partial(_mod_spec, layer, tm=tm, d=d, n_prompt=n_prompt, dec_seq=dec_seq)
    return pl.pallas_call(
        functools.partial(_merge_kernel, prompt_tiles=pt),
        grid=(m_tok // tm,),
        in_specs=[
            pl.BlockSpec((tm, d), lambda i: (i, 0)),
            mspec(m=5),
            pl.BlockSpec((tm, dc), lambda i: (i, 0)),
            pl.BlockSpec((tm, dn), lambda i: (jnp.minimum(i, pt - 1), 0)),
            pl.BlockSpec((tm, dn), lambda i: (jnp.maximum(i - pt, 0), 0)),
            pl.BlockSpec((tm, d), lambda i: (i, 0)),
            pl.BlockSpec((tm, d), lambda i: (i, 1)),
            pl.BlockSpec((None, dc, d), lambda i: (layer, 0, 0)),
            pl.BlockSpec((None, dn, d), lambda i: (layer, 0, 0)),
            pl.BlockSpec((None, d, d), lambda i: (layer, 0, 0)),
        ],
        out_specs=pl.BlockSpec((tm, d), lambda i: (i, 0)),
        out_shape=jax.ShapeDtypeStruct((m_tok, d), F32),
        compiler_params=_params(("parallel",)),
        name=f"merge_l{layer}",
    )(x, mod, conv_h, na_p, na_s, gates, gates, w_conv_out, w_na_out, w_out)


def _final_norm_kernel(x_ref, g_ref, o_ref):
    x = x_ref[...]
    ms = jnp.mean(x * x, axis=-1, keepdims=True)
    o_ref[...] = (x * lax.rsqrt(ms + EPS)) * g_ref[...]


def _final_norm(x, g, row0, nrows, tm=512):
    d = x.shape[1]
    b0 = row0 // tm
    return pl.pallas_call(
        _final_norm_kernel,
        grid=(nrows // tm,),
        in_specs=[pl.BlockSpec((tm, d), lambda i: (b0 + i, 0)), pl.BlockSpec((1, d), lambda i: (0, 0))],
        out_specs=pl.BlockSpec((tm, d), lambda i: (i, 0)),
        out_shape=jax.ShapeDtypeStruct((nrows, d), F32),
        compiler_params=_params(("parallel",)),
        name=f"final_norm_r{row0}",
    )(x, g.reshape(1, d))


def kernel(x_prompt, x_sample, cache_k, cache_v, c, c_ctx, w_ada, b_ada, norm_g, ffn_w_gate, ffn_w_up,
           ffn_w_down, w_in, conv_w, conv_b, conv_ln_g, conv_ln_b, w_conv_out, rel_bias, w_na_out, w_out,
           final_g):
    batch, seq, d = x_prompt.shape
    dec_batch, dec_seq, _ = x_sample.shape
    depth = w_ada.shape[0]
    d_ff = ffn_w_gate.shape[-1]
    d_na = N_HEADS * HEAD_DIM
    dc = conv_w.shape[-1]
    past = cache_k.shape[2]
    n_prompt = batch * seq
    n_latent = dec_batch * dec_seq

    x = jnp.concatenate([x_prompt.reshape(n_prompt, d), x_sample.reshape(n_latent, d)], axis=0)

    cvec = jnp.concatenate([c_ctx[None, :], c, jnp.zeros((MOD_ROWS - 1 - dec_batch, d), F32)], axis=0)
    mod = _ada_table(cvec, w_ada, b_ada).reshape(depth, MOD_ROWS, N_MOD, 1, d)

    ff_pad = (-d_ff) % FFN_CHUNK
    wg = jnp.pad(ffn_w_gate.astype(BF16), ((0, 0), (0, 0), (0, 0), (0, ff_pad)))
    wu = jnp.pad(ffn_w_up.astype(BF16), ((0, 0), (0, 0), (0, 0), (0, ff_pad)))
    wd = jnp.pad(ffn_w_down.astype(BF16), ((0, 0), (0, 0), (0, ff_pad), (0, 0)))
    w_in_b = w_in.astype(BF16)
    w_co_b = w_conv_out.astype(BF16)
    w_no_b = w_na_out.astype(BF16)
    w_o_b = w_out.astype(BF16)
    ck = cache_k.reshape(dec_batch, depth, past, d_na).astype(BF16)
    cv = cache_v.reshape(dec_batch, depth, past, d_na).astype(BF16)
    norm_g4 = norm_g.reshape(depth, 3, 1, d)

    a_cols = 2 * dc + d_na
    rows = dec_seq // GRID_W

    keys, vals = [], []
    for l in range(depth):
        x = _ffn(x, mod, norm_g4, wg, wu, wd, l, 0, 0, n_prompt, dec_seq)

        a, kv, gates = _inproj(x, mod, norm_g4, w_in_b, l, a_cols, 2 * d_na, 2 * d, n_prompt, dec_seq)
        keys.append(kv[:n_prompt, :d_na].reshape(batch, seq, N_HEADS, HEAD_DIM))
        vals.append(kv[:n_prompt, d_na:].reshape(batch, seq, N_HEADS, HEAD_DIM))

        conv_h = _conv_branch(a, conv_w, conv_b, conv_ln_g, conv_ln_b, l, n_prompt, seq, dec_seq)
        na_p = _ctx_attention(a, kv, batch, seq, 2 * dc)
        na_s = _na_attention(a, kv, ck, cv, _na_bias_table(rel_bias[l], rows), l, n_prompt, dec_batch,
                             dec_seq, 2 * dc)
        x = _merge(x, mod, conv_h, na_p, na_s, gates, w_co_b, w_no_b, w_o_b, l, n_prompt, dec_seq)

        x = _ffn(x, mod, norm_g4, wg, wu, wd, l, 1, 6, n_prompt, dec_seq)

    y_prompt = _final_norm(x, final_g, 0, n_prompt).reshape(batch, seq, d)
    y_sample = _final_norm(x, final_g, n_prompt, n_latent).reshape(dec_batch, dec_seq, d)
    return (y_prompt, y_sample, jnp.stack(keys, axis=1), jnp.stack(vals, axis=1))
```

```python
import functools

import jax
import jax.numpy as jnp
from jax import lax
from jax.experimental import pallas as pl
from jax.experimental.pallas import tpu as pltpu

F32 = jnp.float32
BF16 = jnp.bfloat16

N_HEADS = 16
HEAD_DIM = 64
GRID_W = 64
WIN_ROWS = 8
WIN_COLS = 16
CONV_WIDTH = 31
N_MOD = 9
FFN_RES = 0.5
EPS = 1e-6
NEG_INF = -1e30

SUBLANES = 8
LANES = 128
FFN_CHUNK = 512
INPROJ_CHUNK = 512
MOD_ROWS = 8
HALO = 16
VMEM_LIMIT = 56 * 1024 * 1024


def _params(sem, vmem=VMEM_LIMIT):
    return pltpu.CompilerParams(dimension_semantics=sem, vmem_limit_bytes=vmem)


def _silu(x):
    return x * jax.nn.sigmoid(x)


def _prep_kernel(w_ref, o_ref, *, axis, valid, blk):
    w = w_ref[...]
    pos = pl.program_id(1) * blk + lax.broadcasted_iota(jnp.int32, w.shape, axis)
    o_ref[...] = jnp.where(pos < valid, w, 0.0).astype(o_ref.dtype)


def _bf16_col_chunks(w, tn):
    g, k, n = w.shape
    nj = pl.cdiv(n, tn)
    return pl.pallas_call(
        functools.partial(_prep_kernel, axis=1, valid=n, blk=tn),
        grid=(g, nj),
        in_specs=[pl.BlockSpec((None, k, tn), lambda i, j: (i, 0, j))],
        out_specs=pl.BlockSpec((None, None, k, tn), lambda i, j: (i, j, 0, 0)),
        out_shape=jax.ShapeDtypeStruct((g, nj, k, tn), BF16),
        compiler_params=_params(("parallel", "parallel")),
        name="bf16_col_chunks",
    )(w)


def _bf16_row_pad(w, tr):
    g, r, n = w.shape
    nj = pl.cdiv(r, tr)
    return pl.pallas_call(
        functools.partial(_prep_kernel, axis=0, valid=r, blk=tr),
        grid=(g, nj),
        in_specs=[pl.BlockSpec((None, tr, n), lambda i, j: (i, j, 0))],
        out_specs=pl.BlockSpec((None, tr, n), lambda i, j: (i, j, 0)),
        out_shape=jax.ShapeDtypeStruct((g, nj * tr, n), BF16),
        compiler_params=_params(("parallel", "parallel")),
        name="bf16_row_pad",
    )(w)


def _ada_kernel(c_ref, w_ref, b_ref, o_ref):
    s = _silu(c_ref[...]).astype(BF16)
    o_ref[...] = jnp.dot(s, w_ref[...].astype(BF16), preferred_element_type=F32) + b_ref[...]


def _ada_table(cvec, w_ada, b_ada):
    depth, d, n = w_ada.shape
    tn = min(1024, d)
    assert n % tn == 0
    return pl.pallas_call(
        _ada_kernel,
        grid=(depth, n // tn),
        in_specs=[
            pl.BlockSpec((MOD_ROWS, d), lambda l, j: (0, 0)),
            pl.BlockSpec((None, d, tn), lambda l, j: (l, 0, j)),
            pl.BlockSpec((None, 1, tn), lambda l, j: (l, 0, j)),
        ],
        out_specs=pl.BlockSpec((None, MOD_ROWS, tn), lambda l, j: (l, 0, j)),
        out_shape=jax.ShapeDtypeStruct((depth, MOD_ROWS, n), F32),
        compiler_params=_params(("arbitrary", "arbitrary")),
        name="ada_table",
    )(cvec, w_ada, b_ada.reshape(depth, 1, n))


def _mod_spec(layer, m, tm, d, n_prompt, dec_seq):
    def idx(i, *_):
        row0 = i * tm
        grp = jnp.where(row0 < n_prompt, 0, 1 + (row0 - n_prompt) // dec_seq)
        return (layer, grp, m, 0, 0)
    return pl.BlockSpec((None, None, None, 1, d), idx)


NORM_ROWS = 16


def _store_norm_mod(x_ref, g_ref, shift_ref, scale_ref, hn_ref):
    gain = g_ref[...] * (1.0 + scale_ref[...])
    shift = shift_ref[...]

    def body(i, carry):
        r = pl.multiple_of(i * NORM_ROWS, NORM_ROWS)
        x = x_ref[pl.ds(r, NORM_ROWS), :]
        ms = jnp.mean(x * x, axis=-1, keepdims=True)
        hn_ref[pl.ds(r, NORM_ROWS), :] = (x * lax.rsqrt(ms + EPS) * gain + shift).astype(hn_ref.dtype)
        return carry

    lax.fori_loop(0, x_ref.shape[0] // NORM_ROWS, body, 0, unroll=8)


def _ffn_kernel(x_ref, shift_ref, scale_ref, gate_ref, g_ref, wg_ref, wu_ref, wd_ref, o_ref, hn_ref,
                *, n_split):
    j = pl.program_id(1)

    @pl.when(j == 0)
    def _():
        _store_norm_mod(x_ref, g_ref, shift_ref, scale_ref, hn_ref)
        o_ref[...] = jnp.zeros_like(o_ref)

    hn = hn_ref[...]
    cw = wg_ref.shape[-1] // n_split
    part = None
    for c in range(n_split):
        sl = slice(c * cw, (c + 1) * cw)
        a = jnp.dot(hn, wg_ref[:, sl], preferred_element_type=F32)
        u = jnp.dot(hn, wu_ref[:, sl], preferred_element_type=F32)
        h = (_silu(a) * u).astype(BF16)
        p = jnp.dot(h, wd_ref[sl, :], preferred_element_type=F32)
        part = p if part is None else part + p
    o_ref[...] += part

    @pl.when(j == pl.num_programs(1) - 1)
    def _():
        o_ref[...] = x_ref[...] + FFN_RES * gate_ref[...] * o_ref[...]


def _ffn(x, mod, norm_g, wg, wu, wd, layer, sub, mod0, n_prompt, dec_seq, tm=512, n_split=2):
    m_tok, d = x.shape
    n_chunks, tf = wg.shape[1], wg.shape[-1]
    ls = 2 * layer + sub
    mspec = functools.partial(_mod_spec, layer, tm=tm, d=d, n_prompt=n_prompt, dec_seq=dec_seq)
    return pl.pallas_call(
        functools.partial(_ffn_kernel, n_split=n_split),
        grid=(m_tok // tm, n_chunks),
        in_specs=[
            pl.BlockSpec((tm, d), lambda i, j: (i, 0)),
            mspec(m=mod0), mspec(m=mod0 + 1), mspec(m=mod0 + 2),
            pl.BlockSpec((None, None, 1, d), lambda i, j: (layer, 2 * sub, 0, 0)),
            pl.BlockSpec((None, None, d, tf), lambda i, j: (ls, j, 0, 0)),
            pl.BlockSpec((None, None, d, tf), lambda i, j: (ls, j, 0, 0)),
            pl.BlockSpec((None, tf, d), lambda i, j: (ls, j, 0)),
        ],
        out_specs=pl.BlockSpec((tm, d), lambda i, j: (i, 0)),
        out_shape=jax.ShapeDtypeStruct((m_tok, d), F32),
        scratch_shapes=[pltpu.VMEM((tm, d), BF16)],
        compiler_params=_params(("parallel", "arbitrary")),
        name=f"ffn_l{layer}_s{sub}",
    )(x, mod, mod, mod, norm_g, wg, wu, wd)


def _inproj_kernel(x_ref, shift_ref, scale_ref, g_ref, w_ref, a_ref, kv_ref, gt_ref, hn_ref,
                   *, na, nkv, n_split):
    j = pl.program_id(1)

    @pl.when(j == 0)
    def _():
        _store_norm_mod(x_ref, g_ref, shift_ref, scale_ref, hn_ref)

    cw = w_ref.shape[-1] // n_split

    def project(o_ref):
        hn = hn_ref[...]
        for c in range(n_split):
            cs = slice(c * cw, (c + 1) * cw)
            o_ref[:, cs] = jnp.dot(hn, w_ref[:, cs], preferred_element_type=F32).astype(o_ref.dtype)

    @pl.when(j < na)
    def _():
        project(a_ref)

    @pl.when(jnp.logical_and(j >= na, j < na + nkv))
    def _():
        project(kv_ref)

    @pl.when(j >= na + nkv)
    def _():
        project(gt_ref)


def _inproj(x, mod, norm_g, w_in, layer, a_cols, kv_cols, g_cols, n_prompt, dec_seq, tm=1024, n_split=2):
    m_tok, d = x.shape
    n_chunks, tn = w_in.shape[1], w_in.shape[-1]
    na, nkv, ng = a_cols // tn, kv_cols // tn, g_cols // tn
    assert na * tn == a_cols and nkv * tn == kv_cols and ng * tn == g_cols and na + nkv + ng == n_chunks
    mspec = functools.partial(_mod_spec, layer, tm=tm, d=d, n_prompt=n_prompt, dec_seq=dec_seq)
    return pl.pallas_call(
        functools.partial(_inproj_kernel, na=na, nkv=nkv, n_split=n_split),
        grid=(m_tok // tm, n_chunks),
        in_specs=[
            pl.BlockSpec((tm, d), lambda i, j: (i, 0)),
            mspec(m=3), mspec(m=4),
            pl.BlockSpec((None, None, 1, d), lambda i, j: (layer, 1, 0, 0)),
            pl.BlockSpec((None, None, d, tn), lambda i, j: (layer, j, 0, 0)),
        ],
        out_specs=[
            pl.BlockSpec((tm, tn), lambda i, j: (i, jnp.minimum(j, na - 1))),
            pl.BlockSpec((tm, tn), lambda i, j: (i, jnp.clip(j - na, 0, nkv - 1))),
            pl.BlockSpec((tm, tn), lambda i, j: (i, jnp.clip(j - na - nkv, 0, ng - 1))),
        ],
        out_shape=[
            jax.ShapeDtypeStruct((m_tok, a_cols), BF16),
            jax.ShapeDtypeStruct((m_tok, kv_cols), F32),
            jax.ShapeDtypeStruct((m_tok, g_cols), BF16),
        ],
        scratch_shapes=[pltpu.VMEM((tm, d), BF16)],
        compiler_params=_params(("parallel", "arbitrary")),
        name=f"inproj_l{layer}",
    )(x, mod, mod, norm_g, w_in)


def _conv_kernel(prev_ref, cur_ref, next_ref, w_ref, b_ref, lg_ref, lb_ref, o_ref, hs_ref, sh_ref, cv_ref,
                 *, tp, dc, prompt_tiles, tiles_per_seq):
    i = pl.program_id(0)

    def glu(ref):
        z = ref[...].astype(F32)
        return z[:, :dc] * jax.nn.sigmoid(z[:, dc:])

    pos = lax.rem(i - prompt_tiles, tiles_per_seq)
    is_latent = i >= prompt_tiles
    has_prev = jnp.logical_and(is_latent, pos != 0)
    has_next = jnp.logical_and(is_latent, pos != tiles_per_seq - 1)
    hs_ref[0:HALO, :] = jnp.where(has_prev, glu(prev_ref), 0.0)
    hs_ref[HALO:HALO + tp, :] = glu(cur_ref)
    hs_ref[HALO + tp:2 * HALO + tp, :] = jnp.where(has_next, glu(next_ref), 0.0)

    base = HALO - CONV_WIDTH // 2
    ext = sh_ref.shape[1]
    for s in range(SUBLANES):
        sh_ref[s] = hs_ref[base + s:base + s + ext, :]

    rc = 32
    for c0 in range(0, dc, LANES):
        for r0 in range(0, tp, rc):
            acc = jnp.zeros((rc, LANES), F32)
            for k in range(CONV_WIDTH):
                st = r0 + SUBLANES * (k // SUBLANES)
                acc = acc + sh_ref[k % SUBLANES, st:st + rc, c0:c0 + LANES] * w_ref[k:k + 1, c0:c0 + LANES]
            cv_ref[r0:r0 + rc, c0:c0 + LANES] = acc

    h = cv_ref[...] + b_ref[...]
    mu = jnp.mean(h, axis=-1, keepdims=True)
    hc = h - mu
    var = jnp.mean(hc * hc, axis=-1, keepdims=True)
    y = hc * lax.rsqrt(var + EPS) * lg_ref[...] + lb_ref[...]
    o_ref[...] = _silu(y).astype(o_ref.dtype)


def _conv_branch(a, conv_w, conv_b, ln_g, ln_b, layer, n_prompt, seq, dec_seq, tp=256):
    m_tok = a.shape[0]
    dc = conv_w.shape[-1]
    assert seq == tp and dec_seq % tp == 0
    hb = tp // HALO
    n_hblk = m_tok // HALO
    kern = functools.partial(_conv_kernel, tp=tp, dc=dc, prompt_tiles=n_prompt // tp,
                             tiles_per_seq=dec_seq // tp)
    vec = pl.BlockSpec((None, 1, dc), lambda i: (layer, 0, 0))
    return pl.pallas_call(
        kern,
        grid=(m_tok // tp,),
        in_specs=[
            pl.BlockSpec((HALO, 2 * dc), lambda i: (jnp.maximum(i * hb - 1, 0), 0)),
            pl.BlockSpec((tp, 2 * dc), lambda i: (i, 0)),
            pl.BlockSpec((HALO, 2 * dc), lambda i: (jnp.minimum((i + 1) * hb, n_hblk - 1), 0)),
            pl.BlockSpec((None, CONV_WIDTH, dc), lambda i: (layer, 0, 0)),
            vec, vec, vec,
        ],
        out_specs=pl.BlockSpec((tp, dc), lambda i: (i, 0)),
        out_shape=jax.ShapeDtypeStruct((m_tok, dc), BF16),
        scratch_shapes=[
            pltpu.VMEM((tp + 2 * HALO, dc), F32),
            pltpu.VMEM((SUBLANES, tp + SUBLANES * ((CONV_WIDTH - 1) // SUBLANES), dc), F32),
            pltpu.VMEM((tp, dc), F32),
        ],
        compiler_params=_params(("parallel",)),
        name=f"conv_l{layer}",
    )(a, a, a, conv_w, conv_b.reshape(-1, 1, dc), ln_g.reshape(-1, 1, dc), ln_b.reshape(-1, 1, dc))


def _dot_nt(a, b):
    return lax.dot_general(a, b, (((1,), (1,)), ((), ())), preferred_element_type=F32)


def _ctx_attn_kernel(q_ref, k_ref, v_ref, o_ref):
    scale = HEAD_DIM ** -0.5
    outs = []
    for h in range(N_HEADS):
        sl = slice(h * HEAD_DIM, (h + 1) * HEAD_DIM)
        q = q_ref[:, sl]
        k = k_ref[:, sl].astype(BF16)
        v = v_ref[:, sl].astype(BF16)
        s = _dot_nt(q, k) * scale
        e = jnp.exp(s - jnp.max(s, axis=-1, keepdims=True))
        p = e / jnp.sum(e, axis=-1, keepdims=True)
        outs.append(jnp.dot(p.astype(BF16), v, preferred_element_type=F32))
    o_ref[...] = jnp.concatenate(outs, axis=-1).astype(o_ref.dtype)


def _ctx_attention(a, kv, batch, seq, q_col0):
    d_na = N_HEADS * HEAD_DIM
    qb = q_col0 // d_na
    return pl.pallas_call(
        _ctx_attn_kernel,
        grid=(batch,),
        in_specs=[
            pl.BlockSpec((seq, d_na), lambda b: (b, qb)),
            pl.BlockSpec((seq, d_na), lambda b: (b, 0)),
            pl.BlockSpec((seq, d_na), lambda b: (b, 1)),
        ],
        out_specs=pl.BlockSpec((seq, d_na), lambda b: (b, 0)),
        out_shape=jax.ShapeDtypeStruct((batch * seq, d_na), BF16),
        compiler_params=_params(("parallel",)),
        name="ctx_attn",
    )(a, kv, kv)


NA_QROWS = 4
NA_KROWS = NA_QROWS + WIN_ROWS


def _na_kernel(q_ref, k_ref, v_ref, ck_ref, cv_ref, bias_ref, o_ref, *, mb, rows):
    n_blk = rows // NA_QROWS
    qn = NA_QROWS * GRID_W
    kn = NA_KROWS * GRID_W
    first = lax.broadcasted_iota(jnp.int32, (qn, 2 * HEAD_DIM), 1) < HEAD_DIM
    ck = ck_ref[...]
    cv = cv_ref[...]
    for mi in range(mb):
        m = pl.program_id(2) * mb + mi
        kr0 = jnp.clip(NA_QROWS * m - WIN_ROWS // 2, 0, rows - NA_KROWS)
        k0 = pl.multiple_of(kr0 * GRID_W, NA_QROWS * GRID_W)
        ku = k_ref[pl.ds(k0, kn), :].astype(BF16)
        vu = v_ref[pl.ds(k0, kn), :].astype(BF16)
        variant = jnp.where(m == 0, 0, jnp.where(m == n_blk - 1, 2, 1))
        q = q_ref[mi * qn:(mi + 1) * qn, :].astype(F32) * (HEAD_DIM ** -0.5)
        out = None
        for hh in range(2):
            own = first if hh == 0 else jnp.logical_not(first)
            qh = jnp.where(own, q, 0.0).astype(BF16)
            s_loc = _dot_nt(qh, ku) + bias_ref[variant, hh]
            s_ctx = _dot_nt(qh, ck)
            mx = jnp.maximum(jnp.max(s_loc, axis=-1, keepdims=True), jnp.max(s_ctx, axis=-1, keepdims=True))
            e_loc = jnp.exp(s_loc - mx)
            e_ctx = jnp.exp(s_ctx - mx)
            denom = jnp.sum(e_loc, axis=-1, keepdims=True) + jnp.sum(e_ctx, axis=-1, keepdims=True)
            o = (jnp.dot(e_loc.astype(BF16), vu, preferred_element_type=F32)
                 + jnp.dot(e_ctx.astype(BF16), cv, preferred_element_type=F32)) / denom
            out = o if hh == 0 else jnp.where(first, out, o)
        o_ref[mi * qn:(mi + 1) * qn, :] = out.astype(o_ref.dtype)


def _na_bias_table(rel_bias, rows):
    n_blk = rows // NA_QROWS
    assert rows % NA_QROWS == 0 and n_blk >= 3 and rows >= NA_KROWS + WIN_ROWS // 2
    w = jnp.arange(GRID_W)[:, None]
    j = jnp.arange(GRID_W)[None, :]
    cs = jnp.clip(w - WIN_COLS // 2, 0, GRID_W - WIN_COLS)
    vcol = (j >= cs) & (j < cs + WIN_COLS)
    dcol = jnp.clip(j - w + (WIN_COLS - 1), 0, 2 * WIN_COLS - 2)
    n_heads = rel_bias.shape[0]
    cols = jnp.where(vcol[None, :, None, :], jnp.transpose(rel_bias[:, :, dcol], (0, 2, 1, 3)), NEG_INF)

    def masked(n):
        return jnp.full((n_heads, GRID_W, n, GRID_W), NEG_INF, F32)

    tabs = []
    for m in (0, 1, n_blk - 1):
        kr0 = min(max(NA_QROWS * m - WIN_ROWS // 2, 0), rows - NA_KROWS)
        blocks = []
        for rq in range(NA_QROWS):
            r = NA_QROWS * m + rq
            rs = min(max(r - WIN_ROWS // 2, 0), rows - WIN_ROWS)
            u0 = rs - kr0
            d0 = rs - r + (WIN_ROWS - 1)
            parts = [masked(u0), cols[:, :, d0:d0 + WIN_ROWS, :], masked(NA_KROWS - WIN_ROWS - u0)]
            blk = jnp.concatenate([p for p in parts if p.shape[2]], axis=2)
            blocks.append(blk.reshape(n_heads, GRID_W, NA_KROWS * GRID_W))
        tabs.append(jnp.concatenate(blocks, axis=1))
    return jnp.stack(tabs, axis=0)


def _na_attention(a, kv, ck, cv, bias, layer, n_prompt, dec_batch, dec_seq, q_col0, mb=8):
    d_na = N_HEADS * HEAD_DIM
    rows = dec_seq // GRID_W
    hpw = 2 * HEAD_DIM
    qrows = mb * NA_QROWS * GRID_W
    steps = dec_seq // qrows
    past = ck.shape[2]
    assert n_prompt % qrows == 0 and n_prompt % dec_seq == 0 and dec_seq % qrows == 0
    kern = functools.partial(_na_kernel, mb=mb, rows=rows)
    q_blk0 = n_prompt // qrows
    kv_blk0 = n_prompt // dec_seq
    return pl.pallas_call(
        kern,
        grid=(N_HEADS // 2, dec_batch, steps),
        in_specs=[
            pl.BlockSpec((qrows, hpw), lambda hp, b, r: (q_blk0 + b * steps + r, q_col0 // hpw + hp)),
            pl.BlockSpec((dec_seq, hpw), lambda hp, b, r: (kv_blk0 + b, hp)),
            pl.BlockSpec((dec_seq, hpw), lambda hp, b, r: (kv_blk0 + b, d_na // hpw + hp)),
            pl.BlockSpec((None, None, past, hpw), lambda hp, b, r: (b, layer, 0, hp)),
            pl.BlockSpec((None, None, past, hpw), lambda hp, b, r: (b, layer, 0, hp)),
            pl.BlockSpec((3, 2, NA_QROWS * GRID_W, NA_KROWS * GRID_W), lambda hp, b, r: (0, hp, 0, 0)),
        ],
        out_specs=pl.BlockSpec((qrows, hpw), lambda hp, b, r: (b * steps + r, hp)),
        out_shape=jax.ShapeDtypeStruct((dec_batch * dec_seq, d_na), BF16),
        compiler_params=_params(("parallel", "parallel", "arbitrary")),
        name=f"na_l{layer}",
    )(a, kv, kv, ck, cv, bias)


def _merge_kernel(x_ref, gate_ref, ch_ref, nop_ref, nos_ref, gc_ref, gn_ref, wc_ref, wn_ref, wo_ref, o_ref,
                  *, prompt_tiles):
    no = jnp.where(pl.program_id(0) < prompt_tiles, nop_ref[...], nos_ref[...])
    yc = jnp.dot(ch_ref[...], wc_ref[...], preferred_element_type=F32)
    yn = jnp.dot(no, wn_ref[...], preferred_element_type=F32)
    m = (jax.nn.sigmoid(gc_ref[...].astype(F32)) * yc + jax.nn.sigmoid(gn_ref[...].astype(F32)) * yn)
    o_ref[...] = x_ref[...] + gate_ref[...] * jnp.dot(m.astype(BF16), wo_ref[...], preferred_element_type=F32)


def _merge(x, mod, conv_h, na_p, na_s, gates, w_conv_out, w_na_out, w_out, layer, n_prompt, dec_seq, tm=256):
    m_tok, d = x.shape
    dc = conv_h.shape[1]
    dn = na_p.shape[1]
    pt = n_prompt // tm
    mspec = functools.partial(_mod_spec, layer, tm=tm, d=d, n_prompt=n_prompt, dec_seq=dec_seq)
    return pl.pallas_call(
        functools.partial(_merge_kernel, prompt_tiles=pt),
        grid=(m_tok // tm,),
        in_specs=[
            pl.BlockSpec((tm, d), lambda i: (i, 0)),
            mspec(m=5),
            pl.BlockSpec((tm, dc), lambda i: (i, 0)),
            pl.BlockSpec((tm, dn), lambda i: (jnp.minimum(i, pt - 1), 0)),
            pl.BlockSpec((tm, dn), lambda i: (jnp.maximum(i - pt, 0), 0)),
            pl.BlockSpec((tm, d), lambda i: (i, 0)),
            pl.BlockSpec((tm, d), lambda i: (i, 1)),
            pl.BlockSpec((None, dc, d), lambda i: (layer, 0, 0)),
            pl.BlockSpec((None, dn, d), lambda i: (layer, 0, 0)),
            pl.BlockSpec((None, d, d), lambda i: (layer, 0, 0)),
        ],
        out_specs=pl.BlockSpec((tm, d), lambda i: (i, 0)),
        out_shape=jax.ShapeDtypeStruct((m_tok, d), F32),
        compiler_params=_params(("parallel",)),
        name=f"merge_l{layer}",
    )(x, mod, conv_h, na_p, na_s, gates, gates, w_conv_out, w_na_out, w_out)


def _final_norm_kernel(x_ref, g_ref, o_ref):
    x = x_ref[...]
    ms = jnp.mean(x * x, axis=-1, keepdims=True)
    o_ref[...] = (x * lax.rsqrt(ms + EPS)) * g_ref[...]


def _final_norm(x, g, row0, nrows, tm=512):
    d = x.shape[1]
    b0 = row0 // tm
    return pl.pallas_call(
        _final_norm_kernel,
        grid=(nrows // tm,),
        in_specs=[pl.BlockSpec((tm, d), lambda i: (b0 + i, 0)), pl.BlockSpec((1, d), lambda i: (0, 0))],
        out_specs=pl.BlockSpec((tm, d), lambda i: (i, 0)),
        out_shape=jax.ShapeDtypeStruct((nrows, d), F32),
        compiler_params=_params(("parallel",)),
        name=f"final_norm_r{row0}",
    )(x, g.reshape(1, d))


def kernel(x_prompt, x_sample, cache_k, cache_v, c, c_ctx, w_ada, b_ada, norm_g, ffn_w_gate, ffn_w_up,
           ffn_w_down, w_in, conv_w, conv_b, conv_ln_g, conv_ln_b, w_conv_out, rel_bias, w_na_out, w_out,
           final_g):
    batch, seq, d = x_prompt.shape
    dec_batch, dec_seq, _ = x_sample.shape
    depth = w_ada.shape[0]
    d_ff = ffn_w_gate.shape[-1]
    d_na = N_HEADS * HEAD_DIM
    dc = conv_w.shape[-1]
    past = cache_k.shape[2]
    n_prompt = batch * seq
    n_latent = dec_batch * dec_seq

    x = jnp.concatenate([x_prompt.reshape(n_prompt, d), x_sample.reshape(n_latent, d)], axis=0)

    cvec = jnp.concatenate([c_ctx[None, :], c, jnp.zeros((MOD_ROWS - 1 - dec_batch, d), F32)], axis=0)
    mod = _ada_table(cvec, w_ada, b_ada).reshape(depth, MOD_ROWS, N_MOD, 1, d)

    wg = _bf16_col_chunks(ffn_w_gate.reshape(depth * 2, d, d_ff), FFN_CHUNK)
    wu = _bf16_col_chunks(ffn_w_up.reshape(depth * 2, d, d_ff), FFN_CHUNK)
    wd = _bf16_row_pad(ffn_w_down.reshape(depth * 2, d_ff, d), FFN_CHUNK)
    w_in_b = _bf16_col_chunks(w_in, INPROJ_CHUNK)
    w_co_b = w_conv_out.astype(BF16)
    w_no_b = w_na_out.astype(BF16)
    w_o_b = w_out.astype(BF16)
    ck = cache_k.reshape(dec_batch, depth, past, d_na).astype(BF16)
    cv = cache_v.reshape(dec_batch, depth, past, d_na).astype(BF16)
    norm_g4 = norm_g.reshape(depth, 3, 1, d)

    a_cols = 2 * dc + d_na
    rows = dec_seq // GRID_W

    keys, vals = [], []
    for l in range(depth):
        x = _ffn(x, mod, norm_g4, wg, wu, wd, l, 0, 0, n_prompt, dec_seq)

        a, kv, gates = _inproj(x, mod, norm_g4, w_in_b, l, a_cols, 2 * d_na, 2 * d, n_prompt, dec_seq)
        keys.append(kv[:n_prompt, :d_na].reshape(batch, seq, N_HEADS, HEAD_DIM))
        vals.append(kv[:n_prompt, d_na:].reshape(batch, seq, N_HEADS, HEAD_DIM))

        conv_h = _conv_branch(a, conv_w, conv_b, conv_ln_g, conv_ln_b, l, n_prompt, seq, dec_seq)
        na_p = _ctx_attention(a, kv, batch, seq, 2 * dc)
        na_s = _na_attention(a, kv, ck, cv, _na_bias_table(rel_bias[l], rows), l, n_prompt, dec_batch,
                             dec_seq, 2 * dc)
        x = _merge(x, mod, conv_h, na_p, na_s, gates, w_co_b, w_no_b, w_o_b, l, n_prompt, dec_seq)

        x = _ffn(x, mod, norm_g4, wg, wu, wd, l, 1, 6, n_prompt, dec_seq)

    y_prompt = _final_norm(x, final_g, 0, n_prompt).reshape(batch, seq, d)
    y_sample = _final_norm(x, final_g, n_prompt, n_latent).reshape(dec_batch, dec_seq, d)
    return (y_prompt, y_sample, jnp.stack(keys, axis=1), jnp.stack(vals, axis=1))
```

```python
import functools

import jax
import jax.numpy as jnp
from jax import lax
from jax.experimental import pallas as pl
from jax.experimental.pallas import tpu as pltpu

F32 = jnp.float32
BF16 = jnp.bfloat16

N_HEADS = 16
HEAD_DIM = 64
GRID_W = 64
WIN_ROWS = 8
WIN_COLS = 16
CONV_WIDTH = 31
N_MOD = 9
FFN_RES = 0.5
EPS = 1e-6
NEG_INF = -1e30

SUBLANES = 8
LANES = 128
FFN_CHUNK = 512
INPROJ_CHUNK = 512
MOD_ROWS = 8
HALO = 16
VMEM_LIMIT = 56 * 1024 * 1024


def _params(sem, vmem=VMEM_LIMIT):
    return pltpu.CompilerParams(dimension_semantics=sem, vmem_limit_bytes=vmem)


def _silu(x):
    return x * jax.nn.sigmoid(x)


def _prep_kernel(w_ref, o_ref, *, axis, valid, blk):
    w = w_ref[...]
    pos = pl.program_id(1) * blk + lax.broadcasted_iota(jnp.int32, w.shape, axis)
    o_ref[...] = jnp.where(pos < valid, w, 0.0).astype(o_ref.dtype)


def _bf16_col_chunks(w, tn):
    g, k, n = w.shape
    nj = pl.cdiv(n, tn)
    return pl.pallas_call(
        functools.partial(_prep_kernel, axis=1, valid=n, blk=tn),
        grid=(g, nj),
        in_specs=[pl.BlockSpec((None, k, tn), lambda i, j: (i, 0, j))],
        out_specs=pl.BlockSpec((None, None, k, tn), lambda i, j: (i, j, 0, 0)),
        out_shape=jax.ShapeDtypeStruct((g, nj, k, tn), BF16),
        compiler_params=_params(("parallel", "parallel")),
        name="bf16_col_chunks",
    )(w)


def _bf16_row_pad(w, tr):
    g, r, n = w.shape
    nj = pl.cdiv(r, tr)
    return pl.pallas_call(
        functools.partial(_prep_kernel, axis=0, valid=r, blk=tr),
        grid=(g, nj),
        in_specs=[pl.BlockSpec((None, tr, n), lambda i, j: (i, j, 0))],
        out_specs=pl.BlockSpec((None, tr, n), lambda i, j: (i, j, 0)),
        out_shape=jax.ShapeDtypeStruct((g, nj * tr, n), BF16),
        compiler_params=_params(("parallel", "parallel")),
        name="bf16_row_pad",
    )(w)


def _ada_kernel(c_ref, w_ref, b_ref, o_ref):
    s = _silu(c_ref[...]).astype(BF16)
    o_ref[...] = jnp.dot(s, w_ref[...].astype(BF16), preferred_element_type=F32) + b_ref[...]


def _ada_table(cvec, w_ada, b_ada):
    depth, d, n = w_ada.shape
    tn = min(1024, d)
    assert n % tn == 0
    return pl.pallas_call(
        _ada_kernel,
        grid=(depth, n // tn),
        in_specs=[
            pl.BlockSpec((MOD_ROWS, d), lambda l, j: (0, 0)),
            pl.BlockSpec((None, d, tn), lambda l, j: (l, 0, j)),
            pl.BlockSpec((None, 1, tn), lambda l, j: (l, 0, j)),
        ],
        out_specs=pl.BlockSpec((None, MOD_ROWS, tn), lambda l, j: (l, 0, j)),
        out_shape=jax.ShapeDtypeStruct((depth, MOD_ROWS, n), F32),
        compiler_params=_params(("arbitrary", "arbitrary")),
        name="ada_table",
    )(cvec, w_ada, b_ada.reshape(depth, 1, n))


def _mod_spec(layer, m, tm, d, n_prompt, dec_seq):
    def idx(i, *_):
        row0 = i * tm
        grp = jnp.where(row0 < n_prompt, 0, 1 + (row0 - n_prompt) // dec_seq)
        return (layer, grp, m, 0, 0)
    return pl.BlockSpec((None, None, None, 1, d), idx)


NORM_ROWS = 16


def _store_norm_mod(x_ref, g_ref, shift_ref, scale_ref, hn_ref):
    gain = g_ref[...] * (1.0 + scale_ref[...])
    shift = shift_ref[...]

    def body(i, carry):
        r = pl.multiple_of(i * NORM_ROWS, NORM_ROWS)
        x = x_ref[pl.ds(r, NORM_ROWS), :]
        ms = jnp.mean(x * x, axis=-1, keepdims=True)
        hn_ref[pl.ds(r, NORM_ROWS), :] = (x * lax.rsqrt(ms + EPS) * gain + shift).astype(hn_ref.dtype)
        return carry

    lax.fori_loop(0, x_ref.shape[0] // NORM_ROWS, body, 0, unroll=8)


def _ffn_kernel(*refs, n_split, n_x, prompt_tiles, final):
    x_refs, refs = refs[:n_x], refs[n_x:]
    shift_ref, scale_ref, gate_ref, g_ref, wg_ref, wu_ref, wd_ref = refs[:7]
    if final:
        fg_ref, yp_ref, ys_ref, hn_ref, acc_ref = refs[7:]
    else:
        o_ref, hn_ref = refs[7:]
        acc_ref = o_ref
    i = pl.program_id(0)
    j = pl.program_id(1)

    def per_stream(fn_prompt, fn_latent):
        pl.when(i < prompt_tiles)(fn_prompt)
        pl.when(i >= prompt_tiles)(fn_latent)

    def with_x(fn):
        if n_x == 1:
            fn(x_refs[0])
        else:
            per_stream(lambda: fn(x_refs[0]), lambda: fn(x_refs[1]))

    @pl.when(j == 0)
    def _():
        with_x(lambda x_ref: _store_norm_mod(x_ref, g_ref, shift_ref, scale_ref, hn_ref))
        acc_ref[...] = jnp.zeros_like(acc_ref)

    hn = hn_ref[...]
    cw = wg_ref.shape[-1] // n_split
    part = None
    for c in range(n_split):
        sl = slice(c * cw, (c + 1) * cw)
        a = jnp.dot(hn, wg_ref[:, sl], preferred_element_type=F32)
        u = jnp.dot(hn, wu_ref[:, sl], preferred_element_type=F32)
        h = (_silu(a) * u).astype(BF16)
        p = jnp.dot(h, wd_ref[sl, :], preferred_element_type=F32)
        part = p if part is None else part + p
    acc_ref[...] += part

    @pl.when(j == pl.num_programs(1) - 1)
    def _():
        def finish(x_ref):
            y = x_ref[...] + FFN_RES * gate_ref[...] * acc_ref[...]
            if not final:
                o_ref[...] = y
                return
            ms = jnp.mean(y * y, axis=-1, keepdims=True)
            yn = (y * lax.rsqrt(ms + EPS)) * fg_ref[...]

            def to_prompt():
                yp_ref[...] = yn

            def to_latent():
                ys_ref[...] = yn

            per_stream(to_prompt, to_latent)

        with_x(finish)


def _ffn(xs, mod, norm_g, wg, wu, wd, layer, sub, mod0, n_prompt, dec_seq, final_g=None, tm=512, n_split=2):
    d = xs[0].shape[1]
    m_tok = sum(x.shape[0] for x in xs)
    n_chunks, tf = wg.shape[1], wg.shape[-1]
    ls = 2 * layer + sub
    pt = n_prompt // tm
    mspec = functools.partial(_mod_spec, layer, tm=tm, d=d, n_prompt=n_prompt, dec_seq=dec_seq)
    split_specs = [pl.BlockSpec((tm, d), lambda i, j: (jnp.minimum(i, pt - 1), 0)),
                   pl.BlockSpec((tm, d), lambda i, j: (jnp.maximum(i - pt, 0), 0))]
    whole_spec = pl.BlockSpec((tm, d), lambda i, j: (i, 0))
    in_specs = (split_specs if len(xs) == 2 else [whole_spec]) + [
        mspec(m=mod0), mspec(m=mod0 + 1), mspec(m=mod0 + 2),
        pl.BlockSpec((None, None, 1, d), lambda i, j: (layer, 2 * sub, 0, 0)),
        pl.BlockSpec((None, None, d, tf), lambda i, j: (ls, j, 0, 0)),
        pl.BlockSpec((None, None, d, tf), lambda i, j: (ls, j, 0, 0)),
        pl.BlockSpec((None, tf, d), lambda i, j: (ls, j, 0)),
    ]
    args = list(xs) + [mod, mod, mod, norm_g, wg, wu, wd]
    scratch = [pltpu.VMEM((tm, d), BF16)]
    if final_g is None:
        out_specs = whole_spec
        out_shape = jax.ShapeDtypeStruct((m_tok, d), F32)
    else:
        in_specs.append(pl.BlockSpec((1, d), lambda i, j: (0, 0)))
        args.append(final_g.reshape(1, d))
        out_specs = split_specs
        out_shape = [jax.ShapeDtypeStruct((n_prompt, d), F32), jax.ShapeDtypeStruct((m_tok - n_prompt, d), F32)]
        scratch.append(pltpu.VMEM((tm, d), F32))
    return pl.pallas_call(
        functools.partial(_ffn_kernel, n_split=n_split, n_x=len(xs), prompt_tiles=pt, final=final_g is not None),
        grid=(m_tok // tm, n_chunks),
        in_specs=in_specs,
        out_specs=out_specs,
        out_shape=out_shape,
        scratch_shapes=scratch,
        compiler_params=_params(("arbitrary" if final_g is not None else "parallel", "arbitrary")),
        name=f"ffn_l{layer}_s{sub}",
    )(*args)


def _inproj_kernel(x_ref, shift_ref, scale_ref, g_ref, w_ref, a_ref, kv_ref, gt_ref, hn_ref,
                   *, na, nkv, n_split):
    j = pl.program_id(1)

    @pl.when(j == 0)
    def _():
        _store_norm_mod(x_ref, g_ref, shift_ref, scale_ref, hn_ref)

    cw = w_ref.shape[-1] // n_split

    def project(o_ref):
        hn = hn_ref[...]
        for c in range(n_split):
            cs = slice(c * cw, (c + 1) * cw)
            o_ref[:, cs] = jnp.dot(hn, w_ref[:, cs], preferred_element_type=F32).astype(o_ref.dtype)

    @pl.when(j < na)
    def _():
        project(a_ref)

    @pl.when(jnp.logical_and(j >= na, j < na + nkv))
    def _():
        project(kv_ref)

    @pl.when(j >= na + nkv)
    def _():
        project(gt_ref)


def _inproj(x, mod, norm_g, w_in, layer, a_cols, kv_cols, g_cols, n_prompt, dec_seq, tm=1024, n_split=2):
    m_tok, d = x.shape
    n_chunks, tn = w_in.shape[1], w_in.shape[-1]
    na, nkv, ng = a_cols // tn, kv_cols // tn, g_cols // tn
    assert na * tn == a_cols and nkv * tn == kv_cols and ng * tn == g_cols and na + nkv + ng == n_chunks
    mspec = functools.partial(_mod_spec, layer, tm=tm, d=d, n_prompt=n_prompt, dec_seq=dec_seq)
    return pl.pallas_call(
        functools.partial(_inproj_kernel, na=na, nkv=nkv, n_split=n_split),
        grid=(m_tok // tm, n_chunks),
        in_specs=[
            pl.BlockSpec((tm, d), lambda i, j: (i, 0)),
            mspec(m=3), mspec(m=4),
            pl.BlockSpec((None, None, 1, d), lambda i, j: (layer, 1, 0, 0)),
            pl.BlockSpec((None, None, d, tn), lambda i, j: (layer, j, 0, 0)),
        ],
        out_specs=[
            pl.BlockSpec((tm, tn), lambda i, j: (i, jnp.minimum(j, na - 1))),
            pl.BlockSpec((tm, tn), lambda i, j: (i, jnp.clip(j - na, 0, nkv - 1))),
            pl.BlockSpec((tm, tn), lambda i, j: (i, jnp.clip(j - na - nkv, 0, ng - 1))),
        ],
        out_shape=[
            jax.ShapeDtypeStruct((m_tok, a_cols), BF16),
            jax.ShapeDtypeStruct((m_tok, kv_cols), F32),
            jax.ShapeDtypeStruct((m_tok, g_cols), BF16),
        ],
        scratch_shapes=[pltpu.VMEM((tm, d), BF16)],
        compiler_params=_params(("parallel", "arbitrary")),
        name=f"inproj_l{layer}",
    )(x, mod, mod, norm_g, w_in)


def _conv_tile(i, prev_ref, cur_ref, next_ref, w_ref, b_ref, lg_ref, lb_ref, hs_ref, sh_ref, cv_ref,
               *, tp, dc, prompt_tiles, tiles_per_seq):
    def glu(ref):
        z = ref[...].astype(F32)
        return z[:, :dc] * jax.nn.sigmoid(z[:, dc:])

    pos = lax.rem(i - prompt_tiles, tiles_per_seq)
    is_latent = i >= prompt_tiles
    has_prev = jnp.logical_and(is_latent, pos != 0)
    has_next = jnp.logical_and(is_latent, pos != tiles_per_seq - 1)
    hs_ref[0:HALO, :] = jnp.where(has_prev, glu(prev_ref), 0.0)
    hs_ref[HALO:HALO + tp, :] = glu(cur_ref)
    hs_ref[HALO + tp:2 * HALO + tp, :] = jnp.where(has_next, glu(next_ref), 0.0)

    base = HALO - CONV_WIDTH // 2
    ext = sh_ref.shape[1]
    for s in range(SUBLANES):
        sh_ref[s] = hs_ref[base + s:base + s + ext, :]

    rc = 32
    for c0 in range(0, dc, LANES):
        for r0 in range(0, tp, rc):
            acc = jnp.zeros((rc, LANES), F32)
            for k in range(CONV_WIDTH):
                st = r0 + SUBLANES * (k // SUBLANES)
                acc = acc + sh_ref[k % SUBLANES, st:st + rc, c0:c0 + LANES] * w_ref[k:k + 1, c0:c0 + LANES]
            cv_ref[r0:r0 + rc, c0:c0 + LANES] = acc

    h = cv_ref[...] + b_ref[...]
    mu = jnp.mean(h, axis=-1, keepdims=True)
    hc = h - mu
    var = jnp.mean(hc * hc, axis=-1, keepdims=True)
    y = hc * lax.rsqrt(var + EPS) * lg_ref[...] + lb_ref[...]
    return _silu(y)


def _dot_nt(a, b):
    return lax.dot_general(a, b, (((1,), (1,)), ((), ())), preferred_element_type=F32)


def _ctx_attn_kernel(q_ref, k_ref, v_ref, o_ref):
    scale = HEAD_DIM ** -0.5
    outs = []
    for h in range(N_HEADS):
        sl = slice(h * HEAD_DIM, (h + 1) * HEAD_DIM)
        q = q_ref[:, sl]
        k = k_ref[:, sl].astype(BF16)
        v = v_ref[:, sl].astype(BF16)
        s = _dot_nt(q, k) * scale
        e = jnp.exp(s - jnp.max(s, axis=-1, keepdims=True))
        p = e / jnp.sum(e, axis=-1, keepdims=True)
        outs.append(jnp.dot(p.astype(BF16), v, preferred_element_type=F32))
    o_ref[...] = jnp.concatenate(outs, axis=-1).astype(o_ref.dtype)


def _ctx_attention(a, kv, batch, seq, q_col0):
    d_na = N_HEADS * HEAD_DIM
    qb = q_col0 // d_na
    return pl.pallas_call(
        _ctx_attn_kernel,
        grid=(batch,),
        in_specs=[
            pl.BlockSpec((seq, d_na), lambda b: (b, qb)),
            pl.BlockSpec((seq, d_na), lambda b: (b, 0)),
            pl.BlockSpec((seq, d_na), lambda b: (b, 1)),
        ],
        out_specs=pl.BlockSpec((seq, d_na), lambda b: (b, 0)),
        out_shape=jax.ShapeDtypeStruct((batch * seq, d_na), BF16),
        compiler_params=_params(("parallel",)),
        name="ctx_attn",
    )(a, kv, kv)


NA_QROWS = 4
NA_KROWS = NA_QROWS + WIN_ROWS


def _na_kernel(q_ref, k_ref, v_ref, ck_ref, cv_ref, bias_ref, o_ref, *, mb, rows):
    n_blk = rows // NA_QROWS
    qn = NA_QROWS * GRID_W
    kn = NA_KROWS * GRID_W
    first = lax.broadcasted_iota(jnp.int32, (qn, 2 * HEAD_DIM), 1) < HEAD_DIM
    ck = ck_ref[...]
    cv = cv_ref[...]
    for mi in range(mb):
        m = pl.program_id(2) * mb + mi
        kr0 = jnp.clip(NA_QROWS * m - WIN_ROWS // 2, 0, rows - NA_KROWS)
        k0 = pl.multiple_of(kr0 * GRID_W, NA_QROWS * GRID_W)
        ku = k_ref[pl.ds(k0, kn), :].astype(BF16)
        vu = v_ref[pl.ds(k0, kn), :].astype(BF16)
        variant = jnp.where(m == 0, 0, jnp.where(m == n_blk - 1, 2, 1))
        q = q_ref[mi * qn:(mi + 1) * qn, :].astype(F32) * (HEAD_DIM ** -0.5)
        out = None
        for hh in range(2):
            own = first if hh == 0 else jnp.logical_not(first)
            qh = jnp.where(own, q, 0.0).astype(BF16)
            s_loc = _dot_nt(qh, ku) + bias_ref[variant, hh]
            s_ctx = _dot_nt(qh, ck)
            mx = jnp.maximum(jnp.max(s_loc, axis=-1, keepdims=True), jnp.max(s_ctx, axis=-1, keepdims=True))
            e_loc = jnp.exp(s_loc - mx)
            e_ctx = jnp.exp(s_ctx - mx)
            denom = jnp.sum(e_loc, axis=-1, keepdims=True) + jnp.sum(e_ctx, axis=-1, keepdims=True)
            o = (jnp.dot(e_loc.astype(BF16), vu, preferred_element_type=F32)
                 + jnp.dot(e_ctx.astype(BF16), cv, preferred_element_type=F32)) / denom
            out = o if hh == 0 else jnp.where(first, out, o)
        o_ref[mi * qn:(mi + 1) * qn, :] = out.astype(o_ref.dtype)


def _na_bias_table(rel_bias, rows):
    n_blk = rows // NA_QROWS
    assert rows % NA_QROWS == 0 and n_blk >= 3 and rows >= NA_KROWS + WIN_ROWS // 2
    w = jnp.arange(GRID_W)[:, None]
    j = jnp.arange(GRID_W)[None, :]
    cs = jnp.clip(w - WIN_COLS // 2, 0, GRID_W - WIN_COLS)
    vcol = (j >= cs) & (j < cs + WIN_COLS)
    dcol = jnp.clip(j - w + (WIN_COLS - 1), 0, 2 * WIN_COLS - 2)
    n_heads = rel_bias.shape[0]
    cols = jnp.where(vcol[None, :, None, :], jnp.transpose(rel_bias[:, :, dcol], (0, 2, 1, 3)), NEG_INF)

    def masked(n):
        return jnp.full((n_heads, GRID_W, n, GRID_W), NEG_INF, F32)

    tabs = []
    for m in (0, 1, n_blk - 1):
        kr0 = min(max(NA_QROWS * m - WIN_ROWS // 2, 0), rows - NA_KROWS)
        blocks = []
        for rq in range(NA_QROWS):
            r = NA_QROWS * m + rq
            rs = min(max(r - WIN_ROWS // 2, 0), rows - WIN_ROWS)
            u0 = rs - kr0
            d0 = rs - r + (WIN_ROWS - 1)
            parts = [masked(u0), cols[:, :, d0:d0 + WIN_ROWS, :], masked(NA_KROWS - WIN_ROWS - u0)]
            blk = jnp.concatenate([p for p in parts if p.shape[2]], axis=2)
            blocks.append(blk.reshape(n_heads, GRID_W, NA_KROWS * GRID_W))
        tabs.append(jnp.concatenate(blocks, axis=1))
    return jnp.stack(tabs, axis=0)


def _na_attention(a, kv, ck, cv, bias, layer, n_prompt, dec_batch, dec_seq, q_col0, mb=8):
    d_na = N_HEADS * HEAD_DIM
    rows = dec_seq // GRID_W
    hpw = 2 * HEAD_DIM
    qrows = mb * NA_QROWS * GRID_W
    steps = dec_seq // qrows
    past = ck.shape[2]
    assert n_prompt % qrows == 0 and n_prompt % dec_seq == 0 and dec_seq % qrows == 0
    kern = functools.partial(_na_kernel, mb=mb, rows=rows)
    q_blk0 = n_prompt // qrows
    kv_blk0 = n_prompt // dec_seq
    return pl.pallas_call(
        kern,
        grid=(N_HEADS // 2, dec_batch, steps),
        in_specs=[
            pl.BlockSpec((qrows, hpw), lambda hp, b, r: (q_blk0 + b * steps + r, q_col0 // hpw + hp)),
            pl.BlockSpec((dec_seq, hpw), lambda hp, b, r: (kv_blk0 + b, hp)),
            pl.BlockSpec((dec_seq, hpw), lambda hp, b, r: (kv_blk0 + b, d_na // hpw + hp)),
            pl.BlockSpec((None, None, past, hpw), lambda hp, b, r: (b, layer, 0, hp)),
            pl.BlockSpec((None, None, past, hpw), lambda hp, b, r: (b, layer, 0, hp)),
            pl.BlockSpec((3, 2, NA_QROWS * GRID_W, NA_KROWS * GRID_W), lambda hp, b, r: (0, hp, 0, 0)),
        ],
        out_specs=pl.BlockSpec((qrows, hpw), lambda hp, b, r: (b * steps + r, hp)),
        out_shape=jax.ShapeDtypeStruct((dec_batch * dec_seq, d_na), BF16),
        compiler_params=_params(("parallel", "parallel", "arbitrary")),
        name=f"na_l{layer}",
    )(a, kv, kv, ck, cv, bias)


def _merge_conv_kernel(x_ref, gate_ref, nop_ref, nos_ref, gc_ref, gn_ref, wc_ref, wn_ref, wo_ref,
                       prev_ref, cur_ref, next_ref, cw_ref, cb_ref, lg_ref, lb_ref,
                       o_ref, hs_ref, sh_ref, cv_ref, ch_ref, *, n_tiles, prompt_tiles, conv_kw):
    s = pl.program_id(0)

    @pl.when(s == 0)
    def _():
        ch_ref[...] = jnp.zeros_like(ch_ref)

    ch = ch_ref[(s + 1) % 2]
    no = jnp.where(jnp.maximum(s - 1, 0) < prompt_tiles, nop_ref[...], nos_ref[...])
    yc = jnp.dot(ch, wc_ref[...], preferred_element_type=F32)
    yn = jnp.dot(no, wn_ref[...], preferred_element_type=F32)
    m = (jax.nn.sigmoid(gc_ref[...].astype(F32)) * yc + jax.nn.sigmoid(gn_ref[...].astype(F32)) * yn)
    o_ref[...] = x_ref[...] + gate_ref[...] * jnp.dot(m.astype(BF16), wo_ref[...], preferred_element_type=F32)

    ch_new = _conv_tile(jnp.minimum(s, n_tiles - 1), prev_ref, cur_ref, next_ref, cw_ref, cb_ref, lg_ref, lb_ref,
                        hs_ref, sh_ref, cv_ref, **conv_kw)
    ch_ref[s % 2] = ch_new.astype(ch_ref.dtype)


def _merge_conv(x, mod, a, na_p, na_s, gates, conv_w, conv_b, ln_g, ln_b, w_conv_out, w_na_out, w_out,
                layer, n_prompt, seq, dec_seq, tm=256):
    m_tok, d = x.shape
    dc = conv_w.shape[-1]
    dn = na_p.shape[1]
    assert seq == tm and dec_seq % tm == 0 and n_prompt % tm == 0
    n_tiles = m_tok // tm
    pt = n_prompt // tm
    hb = tm // HALO
    n_hblk = m_tok // HALO

    def mt(s):
        return jnp.maximum(s - 1, 0)

    def ct(s):
        return jnp.minimum(s, n_tiles - 1)

    def gate_idx(s):
        row0 = mt(s) * tm
        return (layer, jnp.where(row0 < n_prompt, 0, 1 + (row0 - n_prompt) // dec_seq), 5, 0, 0)

    vec = pl.BlockSpec((None, 1, dc), lambda s: (layer, 0, 0))
    conv_kw = dict(tp=tm, dc=dc, prompt_tiles=pt, tiles_per_seq=dec_seq // tm)
    return pl.pallas_call(
        functools.partial(_merge_conv_kernel, n_tiles=n_tiles, prompt_tiles=pt, conv_kw=conv_kw),
        grid=(n_tiles + 1,),
        in_specs=[
            pl.BlockSpec((tm, d), lambda s: (mt(s), 0)),
            pl.BlockSpec((None, None, None, 1, d), gate_idx),
            pl.BlockSpec((tm, dn), lambda s: (jnp.minimum(mt(s), pt - 1), 0)),
            pl.BlockSpec((tm, dn), lambda s: (jnp.maximum(mt(s) - pt, 0), 0)),
            pl.BlockSpec((tm, d), lambda s: (mt(s), 0)),
            pl.BlockSpec((tm, d), lambda s: (mt(s), 1)),
            pl.BlockSpec((None, dc, d), lambda s: (layer, 0, 0)),
            pl.BlockSpec((None, dn, d), lambda s: (layer, 0, 0)),
            pl.BlockSpec((None, d, d), lambda s: (layer, 0, 0)),
            pl.BlockSpec((HALO, 2 * dc), lambda s: (jnp.maximum(ct(s) * hb - 1, 0), 0)),
            pl.BlockSpec((tm, 2 * dc), lambda s: (ct(s), 0)),
            pl.BlockSpec((HALO, 2 * dc), lambda s: (jnp.minimum((ct(s) + 1) * hb, n_hblk - 1), 0)),
            pl.BlockSpec((None, CONV_WIDTH, dc), lambda s: (layer, 0, 0)),
            vec, vec, vec,
        ],
        out_specs=pl.BlockSpec((tm, d), lambda s: (mt(s), 0)),
        out_shape=jax.ShapeDtypeStruct((m_tok, d), F32),
        scratch_shapes=[
            pltpu.VMEM((tm + 2 * HALO, dc), F32),
            pltpu.VMEM((SUBLANES, tm + SUBLANES * ((CONV_WIDTH - 1) // SUBLANES), dc), F32),
            pltpu.VMEM((tm, dc), F32),
            pltpu.VMEM((2, tm, dc), BF16),
        ],
        compiler_params=_params(("arbitrary",)),
        name=f"merge_conv_l{layer}",
    )(x, mod, na_p, na_s, gates, gates, w_conv_out, w_na_out, w_out, a, a, a, conv_w,
      conv_b.reshape(-1, 1, dc), ln_g.reshape(-1, 1, dc), ln_b.reshape(-1, 1, dc))


def kernel(x_prompt, x_sample, cache_k, cache_v, c, c_ctx, w_ada, b_ada, norm_g, ffn_w_gate, ffn_w_up,
           ffn_w_down, w_in, conv_w, conv_b, conv_ln_g, conv_ln_b, w_conv_out, rel_bias, w_na_out, w_out,
           final_g):
    batch, seq, d = x_prompt.shape
    dec_batch, dec_seq, _ = x_sample.shape
    depth = w_ada.shape[0]
    d_ff = ffn_w_gate.shape[-1]
    d_na = N_HEADS * HEAD_DIM
    dc = conv_w.shape[-1]
    past = cache_k.shape[2]
    n_prompt = batch * seq
    n_latent = dec_batch * dec_seq

    cvec = jnp.concatenate([c_ctx[None, :], c, jnp.zeros((MOD_ROWS - 1 - dec_batch, d), F32)], axis=0)
    mod = _ada_table(cvec, w_ada, b_ada).reshape(depth, MOD_ROWS, N_MOD, 1, d)

    wg = _bf16_col_chunks(ffn_w_gate.reshape(depth * 2, d, d_ff), FFN_CHUNK)
    wu = _bf16_col_chunks(ffn_w_up.reshape(depth * 2, d, d_ff), FFN_CHUNK)
    wd = _bf16_row_pad(ffn_w_down.reshape(depth * 2, d_ff, d), FFN_CHUNK)
    w_in_b = _bf16_col_chunks(w_in, INPROJ_CHUNK)
    w_co_b = w_conv_out.astype(BF16)
    w_no_b = w_na_out.astype(BF16)
    w_o_b = w_out.astype(BF16)
    ck = cache_k.reshape(dec_batch, depth, past, d_na).astype(BF16)
    cv = cache_v.reshape(dec_batch, depth, past, d_na).astype(BF16)
    norm_g4 = norm_g.reshape(depth, 3, 1, d)

    a_cols = 2 * dc + d_na
    rows = dec_seq // GRID_W

    xs = (x_prompt.reshape(n_prompt, d), x_sample.reshape(n_latent, d))
    keys, vals = [], []
    for l in range(depth):
        x = _ffn(xs, mod, norm_g4, wg, wu, wd, l, 0, 0, n_prompt, dec_seq)

        a, kv, gates = _inproj(x, mod, norm_g4, w_in_b, l, a_cols, 2 * d_na, 2 * d, n_prompt, dec_seq)
        keys.append(kv[:n_prompt, :d_na].reshape(batch, seq, N_HEADS, HEAD_DIM))
        vals.append(kv[:n_prompt, d_na:].reshape(batch, seq, N_HEADS, HEAD_DIM))

        na_p = _ctx_attention(a, kv, batch, seq, 2 * dc)
        na_s = _na_attention(a, kv, ck, cv, _na_bias_table(rel_bias[l], rows), l, n_prompt, dec_batch,
                             dec_seq, 2 * dc)
        x = _merge_conv(x, mod, a, na_p, na_s, gates, conv_w, conv_b, conv_ln_g, conv_ln_b, w_co_b, w_no_b,
                        w_o_b, l, n_prompt, seq, dec_seq)

        out = _ffn((x,), mod, norm_g4, wg, wu, wd, l, 1, 6, n_prompt, dec_seq,
                   final_g=final_g if l == depth - 1 else None)
        xs = (out,)

    y_prompt, y_sample = out
    return (y_prompt.reshape(batch, seq, d), y_sample.reshape(dec_batch, dec_seq, d),
            jnp.stack(keys, axis=1), jnp.stack(vals, axis=1))
```

```python
import functools

import jax
import jax.numpy as jnp
from jax import lax
from jax.experimental import pallas as pl
from jax.experimental.pallas import tpu as pltpu

F32 = jnp.float32
BF16 = jnp.bfloat16

N_HEADS = 16
HEAD_DIM = 64
GRID_W = 64
WIN_ROWS = 8
WIN_COLS = 16
CONV_WIDTH = 31
N_MOD = 9
FFN_RES = 0.5
EPS = 1e-6
NEG_INF = -1e30

SUBLANES = 8
LANES = 128
FFN_CHUNK = 512
INPROJ_CHUNK = 512
MOD_ROWS = 8
HALO = 16
VMEM_LIMIT = 56 * 1024 * 1024


def _params(sem, vmem=VMEM_LIMIT):
    return pltpu.CompilerParams(dimension_semantics=sem, vmem_limit_bytes=vmem)


def _silu(x):
    return x * jax.nn.sigmoid(x)


def _prep_kernel(w_ref, o_ref, *, axis, valid, blk):
    w = w_ref[...]
    pos = pl.program_id(1) * blk + lax.broadcasted_iota(jnp.int32, w.shape, axis)
    o_ref[...] = jnp.where(pos < valid, w, 0.0).astype(o_ref.dtype)


def _bf16_col_chunks(w, tn):
    g, k, n = w.shape
    nj = pl.cdiv(n, tn)
    return pl.pallas_call(
        functools.partial(_prep_kernel, axis=1, valid=n, blk=tn),
        grid=(g, nj),
        in_specs=[pl.BlockSpec((None, k, tn), lambda i, j: (i, 0, j))],
        out_specs=pl.BlockSpec((None, None, k, tn), lambda i, j: (i, j, 0, 0)),
        out_shape=jax.ShapeDtypeStruct((g, nj, k, tn), BF16),
        compiler_params=_params(("parallel", "parallel")),
        name="bf16_col_chunks",
    )(w)


def _bf16_row_pad(w, tr):
    g, r, n = w.shape
    nj = pl.cdiv(r, tr)
    return pl.pallas_call(
        functools.partial(_prep_kernel, axis=0, valid=r, blk=tr),
        grid=(g, nj),
        in_specs=[pl.BlockSpec((None, tr, n), lambda i, j: (i, j, 0))],
        out_specs=pl.BlockSpec((None, tr, n), lambda i, j: (i, j, 0)),
        out_shape=jax.ShapeDtypeStruct((g, nj * tr, n), BF16),
        compiler_params=_params(("parallel", "parallel")),
        name="bf16_row_pad",
    )(w)


def _ada_kernel(c_ref, w_ref, b_ref, o_ref):
    s = _silu(c_ref[...]).astype(BF16)
    o_ref[...] = jnp.dot(s, w_ref[...].astype(BF16), preferred_element_type=F32) + b_ref[...]


def _ada_table(cvec, w_ada, b_ada):
    depth, d, n = w_ada.shape
    tn = min(1024, d)
    assert n % tn == 0
    return pl.pallas_call(
        _ada_kernel,
        grid=(depth, n // tn),
        in_specs=[
            pl.BlockSpec((MOD_ROWS, d), lambda l, j: (0, 0)),
            pl.BlockSpec((None, d, tn), lambda l, j: (l, 0, j)),
            pl.BlockSpec((None, 1, tn), lambda l, j: (l, 0, j)),
        ],
        out_specs=pl.BlockSpec((None, MOD_ROWS, tn), lambda l, j: (l, 0, j)),
        out_shape=jax.ShapeDtypeStruct((depth, MOD_ROWS, n), F32),
        compiler_params=_params(("arbitrary", "arbitrary")),
        name="ada_table",
    )(cvec, w_ada, b_ada.reshape(depth, 1, n))


def _mod_spec(layer, m, tm, d, n_prompt, dec_seq):
    def idx(i, *_):
        row0 = i * tm
        grp = jnp.where(row0 < n_prompt, 0, 1 + (row0 - n_prompt) // dec_seq)
        return (layer, grp, m, 0, 0)
    return pl.BlockSpec((None, None, None, 1, d), idx)


NORM_ROWS = 16


def _store_norm_mod(x_ref, g_ref, shift_ref, scale_ref, hn_ref):
    gain = g_ref[...] * (1.0 + scale_ref[...])
    shift = shift_ref[...]

    def body(i, carry):
        r = pl.multiple_of(i * NORM_ROWS, NORM_ROWS)
        x = x_ref[pl.ds(r, NORM_ROWS), :]
        ms = jnp.mean(x * x, axis=-1, keepdims=True)
        hn_ref[pl.ds(r, NORM_ROWS), :] = (x * lax.rsqrt(ms + EPS) * gain + shift).astype(hn_ref.dtype)
        return carry

    lax.fori_loop(0, x_ref.shape[0] // NORM_ROWS, body, 0, unroll=8)


def _ffn_kernel(*refs, n_split, n_x, prompt_tiles, final):
    x_refs, refs = refs[:n_x], refs[n_x:]
    shift_ref, scale_ref, gate_ref, g_ref, wg_ref, wu_ref, wd_ref = refs[:7]
    if final:
        fg_ref, yp_ref, ys_ref, hn_ref, acc_ref = refs[7:]
    else:
        o_ref, hn_ref = refs[7:]
        acc_ref = o_ref
    i = pl.program_id(0)
    j = pl.program_id(1)

    def per_stream(fn_prompt, fn_latent):
        pl.when(i < prompt_tiles)(fn_prompt)
        pl.when(i >= prompt_tiles)(fn_latent)

    def with_x(fn):
        if n_x == 1:
            fn(x_refs[0])
        else:
            per_stream(lambda: fn(x_refs[0]), lambda: fn(x_refs[1]))

    @pl.when(j == 0)
    def _():
        with_x(lambda x_ref: _store_norm_mod(x_ref, g_ref, shift_ref, scale_ref, hn_ref))
        acc_ref[...] = jnp.zeros_like(acc_ref)

    hn = hn_ref[...]
    cw = wg_ref.shape[-1] // n_split
    part = None
    for c in range(n_split):
        sl = slice(c * cw, (c + 1) * cw)
        a = jnp.dot(hn, wg_ref[:, sl], preferred_element_type=F32)
        u = jnp.dot(hn, wu_ref[:, sl], preferred_element_type=F32)
        h = (_silu(a) * u).astype(BF16)
        p = jnp.dot(h, wd_ref[sl, :], preferred_element_type=F32)
        part = p if part is None else part + p
    acc_ref[...] += part

    @pl.when(j == pl.num_programs(1) - 1)
    def _():
        def finish(x_ref):
            y = x_ref[...] + FFN_RES * gate_ref[...] * acc_ref[...]
            if not final:
                o_ref[...] = y
                return
            ms = jnp.mean(y * y, axis=-1, keepdims=True)
            yn = (y * lax.rsqrt(ms + EPS)) * fg_ref[...]

            def to_prompt():
                yp_ref[...] = yn

            def to_latent():
                ys_ref[...] = yn

            per_stream(to_prompt, to_latent)

        with_x(finish)


def _ffn(xs, mod, norm_g, wg, wu, wd, layer, sub, mod0, n_prompt, dec_seq, final_g=None, tm=512, n_split=2):
    d = xs[0].shape[1]
    m_tok = sum(x.shape[0] for x in xs)
    n_chunks, tf = wg.shape[1], wg.shape[-1]
    ls = 2 * layer + sub
    pt = n_prompt // tm
    mspec = functools.partial(_mod_spec, layer, tm=tm, d=d, n_prompt=n_prompt, dec_seq=dec_seq)
    split_specs = [pl.BlockSpec((tm, d), lambda i, j: (jnp.minimum(i, pt - 1), 0)),
                   pl.BlockSpec((tm, d), lambda i, j: (jnp.maximum(i - pt, 0), 0))]
    whole_spec = pl.BlockSpec((tm, d), lambda i, j: (i, 0))
    in_specs = (split_specs if len(xs) == 2 else [whole_spec]) + [
        mspec(m=mod0), mspec(m=mod0 + 1), mspec(m=mod0 + 2),
        pl.BlockSpec((None, None, 1, d), lambda i, j: (layer, 2 * sub, 0, 0)),
        pl.BlockSpec((None, None, d, tf), lambda i, j: (ls, j, 0, 0)),
        pl.BlockSpec((None, None, d, tf), lambda i, j: (ls, j, 0, 0)),
        pl.BlockSpec((None, tf, d), lambda i, j: (ls, j, 0)),
    ]
    args = list(xs) + [mod, mod, mod, norm_g, wg, wu, wd]
    scratch = [pltpu.VMEM((tm, d), BF16)]
    if final_g is None:
        out_specs = whole_spec
        out_shape = jax.ShapeDtypeStruct((m_tok, d), F32)
    else:
        in_specs.append(pl.BlockSpec((1, d), lambda i, j: (0, 0)))
        args.append(final_g.reshape(1, d))
        out_specs = split_specs
        out_shape = [jax.ShapeDtypeStruct((n_prompt, d), F32), jax.ShapeDtypeStruct((m_tok - n_prompt, d), F32)]
        scratch.append(pltpu.VMEM((tm, d), F32))
    return pl.pallas_call(
        functools.partial(_ffn_kernel, n_split=n_split, n_x=len(xs), prompt_tiles=pt, final=final_g is not None),
        grid=(m_tok // tm, n_chunks),
        in_specs=in_specs,
        out_specs=out_specs,
        out_shape=out_shape,
        scratch_shapes=scratch,
        compiler_params=_params(("arbitrary" if final_g is not None else "parallel", "arbitrary")),
        name=f"ffn_l{layer}_s{sub}",
    )(*args)


W_SLOTS = 3


def _inproj_kernel(x_ref, shift_ref, scale_ref, g_ref, w_hbm, a_ref, kv_ref, gt_ref, hn_ref, w_buf, w_sem,
                   *, layer, na, nkv, n_split):
    i = pl.program_id(0)
    j = pl.program_id(1)
    n_chunks = pl.num_programs(1)
    step = i * n_chunks + j
    total = pl.num_programs(0) * n_chunks

    def weight_copy(s):
        slot = s % W_SLOTS
        return pltpu.make_async_copy(w_hbm.at[layer, s % n_chunks], w_buf.at[slot], w_sem.at[slot])

    @pl.when(step == 0)
    def _():
        for s in range(W_SLOTS - 1):
            weight_copy(s).start()

    @pl.when(step + (W_SLOTS - 1) < total)
    def _():
        weight_copy(step + (W_SLOTS - 1)).start()

    @pl.when(j == 0)
    def _():
        _store_norm_mod(x_ref, g_ref, shift_ref, scale_ref, hn_ref)

    weight_copy(step).wait()
    w_ref = w_buf.at[step % W_SLOTS]

    cw = w_ref.shape[-1] // n_split

    def project(o_ref):
        hn = hn_ref[...]
        for c in range(n_split):
            cs = slice(c * cw, (c + 1) * cw)
            o_ref[:, cs] = jnp.dot(hn, w_ref[:, cs], preferred_element_type=F32).astype(o_ref.dtype)

    @pl.when(j < na)
    def _():
        project(a_ref)

    @pl.when(jnp.logical_and(j >= na, j < na + nkv))
    def _():
        project(kv_ref)

    @pl.when(j >= na + nkv)
    def _():
        project(gt_ref)


def _inproj(x, mod, norm_g, w_in, layer, a_cols, kv_cols, g_cols, n_prompt, dec_seq, tm=1024, n_split=2):
    m_tok, d = x.shape
    n_chunks, tn = w_in.shape[1], w_in.shape[-1]
    na, nkv, ng = a_cols // tn, kv_cols // tn, g_cols // tn
    assert na * tn == a_cols and nkv * tn == kv_cols and ng * tn == g_cols and na + nkv + ng == n_chunks
    mspec = functools.partial(_mod_spec, layer, tm=tm, d=d, n_prompt=n_prompt, dec_seq=dec_seq)
    return pl.pallas_call(
        functools.partial(_inproj_kernel, layer=layer, na=na, nkv=nkv, n_split=n_split),
        grid=(m_tok // tm, n_chunks),
        in_specs=[
            pl.BlockSpec((tm, d), lambda i, j: (i, 0)),
            mspec(m=3), mspec(m=4),
            pl.BlockSpec((None, None, 1, d), lambda i, j: (layer, 1, 0, 0)),
            pl.BlockSpec(memory_space=pl.ANY),
        ],
        out_specs=[
            pl.BlockSpec((tm, tn), lambda i, j: (i, jnp.minimum(j, na - 1))),
            pl.BlockSpec((tm, tn), lambda i, j: (i, jnp.clip(j - na, 0, nkv - 1))),
            pl.BlockSpec((tm, tn), lambda i, j: (i, jnp.clip(j - na - nkv, 0, ng - 1))),
        ],
        out_shape=[
            jax.ShapeDtypeStruct((m_tok, a_cols), BF16),
            jax.ShapeDtypeStruct((m_tok, kv_cols), F32),
            jax.ShapeDtypeStruct((m_tok, g_cols), BF16),
        ],
        scratch_shapes=[
            pltpu.VMEM((tm, d), BF16),
            pltpu.VMEM((W_SLOTS, d, tn), BF16),
            pltpu.SemaphoreType.DMA((W_SLOTS,)),
        ],
        compiler_params=_params(("arbitrary", "arbitrary")),
        name=f"inproj_l{layer}",
    )(x, mod, mod, norm_g, w_in)


def _conv_tile(i, prev_ref, cur_ref, next_ref, w_ref, b_ref, lg_ref, lb_ref, hs_ref, sh_ref, cv_ref,
               *, tp, dc, prompt_tiles, tiles_per_seq):
    def glu(ref):
        z = ref[...].astype(F32)
        return z[:, :dc] * jax.nn.sigmoid(z[:, dc:])

    pos = lax.rem(i - prompt_tiles, tiles_per_seq)
    is_latent = i >= prompt_tiles
    has_prev = jnp.logical_and(is_latent, pos != 0)
    has_next = jnp.logical_and(is_latent, pos != tiles_per_seq - 1)
    hs_ref[0:HALO, :] = jnp.where(has_prev, glu(prev_ref), 0.0)
    hs_ref[HALO:HALO + tp, :] = glu(cur_ref)
    hs_ref[HALO + tp:2 * HALO + tp, :] = jnp.where(has_next, glu(next_ref), 0.0)

    base = HALO - CONV_WIDTH // 2
    ext = sh_ref.shape[1]
    for s in range(SUBLANES):
        sh_ref[s] = hs_ref[base + s:base + s + ext, :]

    rc = 32
    for c0 in range(0, dc, LANES):
        for r0 in range(0, tp, rc):
            acc = jnp.zeros((rc, LANES), F32)
            for k in range(CONV_WIDTH):
                st = r0 + SUBLANES * (k // SUBLANES)
                acc = acc + sh_ref[k % SUBLANES, st:st + rc, c0:c0 + LANES] * w_ref[k:k + 1, c0:c0 + LANES]
            cv_ref[r0:r0 + rc, c0:c0 + LANES] = acc

    h = cv_ref[...] + b_ref[...]
    mu = jnp.mean(h, axis=-1, keepdims=True)
    hc = h - mu
    var = jnp.mean(hc * hc, axis=-1, keepdims=True)
    y = hc * lax.rsqrt(var + EPS) * lg_ref[...] + lb_ref[...]
    return _silu(y)


def _dot_nt(a, b):
    return lax.dot_general(a, b, (((1,), (1,)), ((), ())), preferred_element_type=F32)


def _ctx_attn_kernel(q_ref, k_ref, v_ref, o_ref):
    scale = HEAD_DIM ** -0.5
    outs = []
    for h in range(N_HEADS):
        sl = slice(h * HEAD_DIM, (h + 1) * HEAD_DIM)
        q = q_ref[:, sl]
        k = k_ref[:, sl].astype(BF16)
        v = v_ref[:, sl].astype(BF16)
        s = _dot_nt(q, k) * scale
        e = jnp.exp(s - jnp.max(s, axis=-1, keepdims=True))
        p = e / jnp.sum(e, axis=-1, keepdims=True)
        outs.append(jnp.dot(p.astype(BF16), v, preferred_element_type=F32))
    o_ref[...] = jnp.concatenate(outs, axis=-1).astype(o_ref.dtype)


def _ctx_attention(a, kv, batch, seq, q_col0):
    d_na = N_HEADS * HEAD_DIM
    qb = q_col0 // d_na
    return pl.pallas_call(
        _ctx_attn_kernel,
        grid=(batch,),
        in_specs=[
            pl.BlockSpec((seq, d_na), lambda b: (b, qb)),
            pl.BlockSpec((seq, d_na), lambda b: (b, 0)),
            pl.BlockSpec((seq, d_na), lambda b: (b, 1)),
        ],
        out_specs=pl.BlockSpec((seq, d_na), lambda b: (b, 0)),
        out_shape=jax.ShapeDtypeStruct((batch * seq, d_na), BF16),
        compiler_params=_params(("parallel",)),
        name="ctx_attn",
    )(a, kv, kv)


NA_QROWS = 4
NA_KROWS = NA_QROWS + WIN_ROWS


def _na_kernel(q_ref, k_ref, v_ref, ck_ref, cv_ref, bias_ref, o_ref, *, mb, rows):
    n_blk = rows // NA_QROWS
    qn = NA_QROWS * GRID_W
    kn = NA_KROWS * GRID_W
    first = lax.broadcasted_iota(jnp.int32, (qn, 2 * HEAD_DIM), 1) < HEAD_DIM
    ck = ck_ref[...]
    cv = cv_ref[...]
    for mi in range(mb):
        m = pl.program_id(2) * mb + mi
        kr0 = jnp.clip(NA_QROWS * m - WIN_ROWS // 2, 0, rows - NA_KROWS)
        k0 = pl.multiple_of(kr0 * GRID_W, NA_QROWS * GRID_W)
        ku = k_ref[pl.ds(k0, kn), :].astype(BF16)
        vu = v_ref[pl.ds(k0, kn), :].astype(BF16)
        variant = jnp.where(m == 0, 0, jnp.where(m == n_blk - 1, 2, 1))
        q = q_ref[mi * qn:(mi + 1) * qn, :].astype(F32) * (HEAD_DIM ** -0.5)
        out = None
        for hh in range(2):
            own = first if hh == 0 else jnp.logical_not(first)
            qh = jnp.where(own, q, 0.0).astype(BF16)
            s_loc = _dot_nt(qh, ku) + bias_ref[variant, hh]
            s_ctx = _dot_nt(qh, ck)
            mx = jnp.maximum(jnp.max(s_loc, axis=-1, keepdims=True), jnp.max(s_ctx, axis=-1, keepdims=True))
            e_loc = jnp.exp(s_loc - mx)
            e_ctx = jnp.exp(s_ctx - mx)
            denom = jnp.sum(e_loc, axis=-1, keepdims=True) + jnp.sum(e_ctx, axis=-1, keepdims=True)
            o = (jnp.dot(e_loc.astype(BF16), vu, preferred_element_type=F32)
                 + jnp.dot(e_ctx.astype(BF16), cv, preferred_element_type=F32)) / denom
            out = o if hh == 0 else jnp.where(first, out, o)
        o_ref[mi * qn:(mi + 1) * qn, :] = out.astype(o_ref.dtype)


def _na_bias_table(rel_bias, rows):
    n_blk = rows // NA_QROWS
    assert rows % NA_QROWS == 0 and n_blk >= 3 and rows >= NA_KROWS + WIN_ROWS // 2
    w = jnp.arange(GRID_W)[:, None]
    j = jnp.arange(GRID_W)[None, :]
    cs = jnp.clip(w - WIN_COLS // 2, 0, GRID_W - WIN_COLS)
    vcol = (j >= cs) & (j < cs + WIN_COLS)
    dcol = jnp.clip(j - w + (WIN_COLS - 1), 0, 2 * WIN_COLS - 2)
    n_heads = rel_bias.shape[0]
    cols = jnp.where(vcol[None, :, None, :], jnp.transpose(rel_bias[:, :, dcol], (0, 2, 1, 3)), NEG_INF)

    def masked(n):
        return jnp.full((n_heads, GRID_W, n, GRID_W), NEG_INF, F32)

    tabs = []
    for m in (0, 1, n_blk - 1):
        kr0 = min(max(NA_QROWS * m - WIN_ROWS // 2, 0), rows - NA_KROWS)
        blocks = []
        for rq in range(NA_QROWS):
            r = NA_QROWS * m + rq
            rs = min(max(r - WIN_ROWS // 2, 0), rows - WIN_ROWS)
            u0 = rs - kr0
            d0 = rs - r + (WIN_ROWS - 1)
            parts = [masked(u0), cols[:, :, d0:d0 + WIN_ROWS, :], masked(NA_KROWS - WIN_ROWS - u0)]
            blk = jnp.concatenate([p for p in parts if p.shape[2]], axis=2)
            blocks.append(blk.reshape(n_heads, GRID_W, NA_KROWS * GRID_W))
        tabs.append(jnp.concatenate(blocks, axis=1))
    return jnp.stack(tabs, axis=0)


def _na_attention(a, kv, ck, cv, bias, layer, n_prompt, dec_batch, dec_seq, q_col0, mb=8):
    d_na = N_HEADS * HEAD_DIM
    rows = dec_seq // GRID_W
    hpw = 2 * HEAD_DIM
    qrows = mb * NA_QROWS * GRID_W
    steps = dec_seq // qrows
    past = ck.shape[2]
    assert n_prompt % qrows == 0 and n_prompt % dec_seq == 0 and dec_seq % qrows == 0
    kern = functools.partial(_na_kernel, mb=mb, rows=rows)
    q_blk0 = n_prompt // qrows
    kv_blk0 = n_prompt // dec_seq
    return pl.pallas_call(
        kern,
        grid=(N_HEADS // 2, dec_batch, steps),
        in_specs=[
            pl.BlockSpec((qrows, hpw), lambda hp, b, r: (q_blk0 + b * steps + r, q_col0 // hpw + hp)),
            pl.BlockSpec((dec_seq, hpw), lambda hp, b, r: (kv_blk0 + b, hp)),
            pl.BlockSpec((dec_seq, hpw), lambda hp, b, r: (kv_blk0 + b, d_na // hpw + hp)),
            pl.BlockSpec((None, None, past, hpw), lambda hp, b, r: (b, layer, 0, hp)),
            pl.BlockSpec((None, None, past, hpw), lambda hp, b, r: (b, layer, 0, hp)),
            pl.BlockSpec((3, 2, NA_QROWS * GRID_W, NA_KROWS * GRID_W), lambda hp, b, r: (0, hp, 0, 0)),
        ],
        out_specs=pl.BlockSpec((qrows, hpw), lambda hp, b, r: (b * steps + r, hp)),
        out_shape=jax.ShapeDtypeStruct((dec_batch * dec_seq, d_na), BF16),
        compiler_params=_params(("parallel", "parallel", "arbitrary")),
        name=f"na_l{layer}",
    )(a, kv, kv, ck, cv, bias)


def _merge_conv_kernel(x_ref, gate_ref, nop_ref, nos_ref, gc_ref, gn_ref, wc_ref, wn_ref, wo_ref,
                       prev_ref, cur_ref, next_ref, cw_ref, cb_ref, lg_ref, lb_ref,
                       o_ref, hs_ref, sh_ref, cv_ref, ch_ref, *, n_tiles, prompt_tiles, conv_kw):
    s = pl.program_id(0)

    @pl.when(s == 0)
    def _():
        ch_ref[...] = jnp.zeros_like(ch_ref)

    ch = ch_ref[(s + 1) % 2]
    no = jnp.where(jnp.maximum(s - 1, 0) < prompt_tiles, nop_ref[...], nos_ref[...])
    yc = jnp.dot(ch, wc_ref[...], preferred_element_type=F32)
    yn = jnp.dot(no, wn_ref[...], preferred_element_type=F32)
    m = (jax.nn.sigmoid(gc_ref[...].astype(F32)) * yc + jax.nn.sigmoid(gn_ref[...].astype(F32)) * yn)
    o_ref[...] = x_ref[...] + gate_ref[...] * jnp.dot(m.astype(BF16), wo_ref[...], preferred_element_type=F32)

    ch_new = _conv_tile(jnp.minimum(s, n_tiles - 1), prev_ref, cur_ref, next_ref, cw_ref, cb_ref, lg_ref, lb_ref,
                        hs_ref, sh_ref, cv_ref, **conv_kw)
    ch_ref[s % 2] = ch_new.astype(ch_ref.dtype)


def _merge_conv(x, mod, a, na_p, na_s, gates, conv_w, conv_b, ln_g, ln_b, w_conv_out, w_na_out, w_out,
                layer, n_prompt, seq, dec_seq, tm=256):
    m_tok, d = x.shape
    dc = conv_w.shape[-1]
    dn = na_p.shape[1]
    assert seq == tm and dec_seq % tm == 0 and n_prompt % tm == 0
    n_tiles = m_tok // tm
    pt = n_prompt // tm
    hb = tm // HALO
    n_hblk = m_tok // HALO

    def mt(s):
        return jnp.maximum(s - 1, 0)

    def ct(s):
        return jnp.minimum(s, n_tiles - 1)

    def gate_idx(s):
        row0 = mt(s) * tm
        return (layer, jnp.where(row0 < n_prompt, 0, 1 + (row0 - n_prompt) // dec_seq), 5, 0, 0)

    vec = pl.BlockSpec((None, 1, dc), lambda s: (layer, 0, 0))
    conv_kw = dict(tp=tm, dc=dc, prompt_tiles=pt, tiles_per_seq=dec_seq // tm)
    return pl.pallas_call(
        functools.partial(_merge_conv_kernel, n_tiles=n_tiles, prompt_tiles=pt, conv_kw=conv_kw),
        grid=(n_tiles + 1,),
        in_specs=[
            pl.BlockSpec((tm, d), lambda s: (mt(s), 0)),
            pl.BlockSpec((None, None, None, 1, d), gate_idx),
            pl.BlockSpec((tm, dn), lambda s: (jnp.minimum(mt(s), pt - 1), 0)),
            pl.BlockSpec((tm, dn), lambda s: (jnp.maximum(mt(s) - pt, 0), 0)),
            pl.BlockSpec((tm, d), lambda s: (mt(s), 0)),
            pl.BlockSpec((tm, d), lambda s: (mt(s), 1)),
            pl.BlockSpec((None, dc, d), lambda s: (layer, 0, 0)),
            pl.BlockSpec((None, dn, d), lambda s: (layer, 0, 0)),
            pl.BlockSpec((None, d, d), lambda s: (layer, 0, 0)),
            pl.BlockSpec((HALO, 2 * dc), lambda s: (jnp.maximum(ct(s) * hb - 1, 0), 0)),
            pl.BlockSpec((tm, 2 * dc), lambda s: (ct(s), 0)),
            pl.BlockSpec((HALO, 2 * dc), lambda s: (jnp.minimum((ct(s) + 1) * hb, n_hblk - 1), 0)),
            pl.BlockSpec((None, CONV_WIDTH, dc), lambda s: (layer, 0, 0)),
            vec, vec, vec,
        ],
        out_specs=pl.BlockSpec((tm, d), lambda s: (mt(s), 0)),
        out_shape=jax.ShapeDtypeStruct((m_tok, d), F32),
        scratch_shapes=[
            pltpu.VMEM((tm + 2 * HALO, dc), F32),
            pltpu.VMEM((SUBLANES, tm + SUBLANES * ((CONV_WIDTH - 1) // SUBLANES), dc), F32),
            pltpu.VMEM((tm, dc), F32),
            pltpu.VMEM((2, tm, dc), BF16),
        ],
        compiler_params=_params(("arbitrary",)),
        name=f"merge_conv_l{layer}",
    )(x, mod, na_p, na_s, gates, gates, w_conv_out, w_na_out, w_out, a, a, a, conv_w,
      conv_b.reshape(-1, 1, dc), ln_g.reshape(-1, 1, dc), ln_b.reshape(-1, 1, dc))


def kernel(x_prompt, x_sample, cache_k, cache_v, c, c_ctx, w_ada, b_ada, norm_g, ffn_w_gate, ffn_w_up,
           ffn_w_down, w_in, conv_w, conv_b, conv_ln_g, conv_ln_b, w_conv_out, rel_bias, w_na_out, w_out,
           final_g):
    batch, seq, d = x_prompt.shape
    dec_batch, dec_seq, _ = x_sample.shape
    depth = w_ada.shape[0]
    d_ff = ffn_w_gate.shape[-1]
    d_na = N_HEADS * HEAD_DIM
    dc = conv_w.shape[-1]
    past = cache_k.shape[2]
    n_prompt = batch * seq
    n_latent = dec_batch * dec_seq

    cvec = jnp.concatenate([c_ctx[None, :], c, jnp.zeros((MOD_ROWS - 1 - dec_batch, d), F32)], axis=0)
    mod = _ada_table(cvec, w_ada, b_ada).reshape(depth, MOD_ROWS, N_MOD, 1, d)

    wg = _bf16_col_chunks(ffn_w_gate.reshape(depth * 2, d, d_ff), FFN_CHUNK)
    wu = _bf16_col_chunks(ffn_w_up.reshape(depth * 2, d, d_ff), FFN_CHUNK)
    wd = _bf16_row_pad(ffn_w_down.reshape(depth * 2, d_ff, d), FFN_CHUNK)
    w_in_b = _bf16_col_chunks(w_in, INPROJ_CHUNK)
    w_co_b = w_conv_out.astype(BF16)
    w_no_b = w_na_out.astype(BF16)
    w_o_b = w_out.astype(BF16)
    ck = cache_k.reshape(dec_batch, depth, past, d_na).astype(BF16)
    cv = cache_v.reshape(dec_batch, depth, past, d_na).astype(BF16)
    norm_g4 = norm_g.reshape(depth, 3, 1, d)

    a_cols = 2 * dc + d_na
    rows = dec_seq // GRID_W

    xs = (x_prompt.reshape(n_prompt, d), x_sample.reshape(n_latent, d))
    keys, vals = [], []
    for l in range(depth):
        x = _ffn(xs, mod, norm_g4, wg, wu, wd, l, 0, 0, n_prompt, dec_seq)

        a, kv, gates = _inproj(x, mod, norm_g4, w_in_b, l, a_cols, 2 * d_na, 2 * d, n_prompt, dec_seq)
        keys.append(kv[:n_prompt, :d_na].reshape(batch, seq, N_HEADS, HEAD_DIM))
        vals.append(kv[:n_prompt, d_na:].reshape(batch, seq, N_HEADS, HEAD_DIM))

        na_p = _ctx_attention(a, kv, batch, seq, 2 * dc)
        na_s = _na_attention(a, kv, ck, cv, _na_bias_table(rel_bias[l], rows), l, n_prompt, dec_batch,
                             dec_seq, 2 * dc)
        x = _merge_conv(x, mod, a, na_p, na_s, gates, conv_w, conv_b, conv_ln_g, conv_ln_b, w_co_b, w_no_b,
                        w_o_b, l, n_prompt, seq, dec_seq)

        out = _ffn((x,), mod, norm_g4, wg, wu, wd, l, 1, 6, n_prompt, dec_seq,
                   final_g=final_g if l == depth - 1 else None)
        xs = (out,)

    y_prompt, y_sample = out
    return (y_prompt.reshape(batch, seq, d), y_sample.reshape(dec_batch, dec_seq, d),
            jnp.stack(keys, axis=1), jnp.stack(vals, axis=1))
```

```python
import functools

import jax
import jax.numpy as jnp
from jax import lax
from jax.experimental import pallas as pl
from jax.experimental.pallas import tpu as pltpu

F32 = jnp.float32
BF16 = jnp.bfloat16

N_HEADS = 16
HEAD_DIM = 64
GRID_W = 64
WIN_ROWS = 8
WIN_COLS = 16
CONV_WIDTH = 31
N_MOD = 9
FFN_RES = 0.5
EPS = 1e-6
NEG_INF = -1e30

SUBLANES = 8
LANES = 128
FFN_CHUNK = 512
INPROJ_CHUNK = 512
ADA_CHUNK = 1024
MOD_ROWS = 8
NORM_ROWS = 16
HALO = 16
VMEM_LIMIT = 56 * 1024 * 1024


def _params(sem, vmem=VMEM_LIMIT):
    return pltpu.CompilerParams(dimension_semantics=sem, vmem_limit_bytes=vmem)


def _silu(x):
    return x * jax.nn.sigmoid(x)


def _prep_kernel(w_ref, o_ref, *, axis, valid, blk):
    w = w_ref[...]
    pos = pl.program_id(1) * blk + lax.broadcasted_iota(jnp.int32, w.shape, axis)
    o_ref[...] = jnp.where(pos < valid, w, 0.0).astype(o_ref.dtype)


def _bf16_col_chunks(w, tn):
    g, k, n = w.shape
    nj = pl.cdiv(n, tn)
    return pl.pallas_call(
        functools.partial(_prep_kernel, axis=1, valid=n, blk=tn),
        grid=(g, nj),
        in_specs=[pl.BlockSpec((None, k, tn), lambda i, j: (i, 0, j))],
        out_specs=pl.BlockSpec((None, None, k, tn), lambda i, j: (i, j, 0, 0)),
        out_shape=jax.ShapeDtypeStruct((g, nj, k, tn), BF16),
        compiler_params=_params(("parallel", "parallel")),
        name="bf16_col_chunks",
    )(w)


def _bf16_row_pad(w, tr):
    g, r, n = w.shape
    nj = pl.cdiv(r, tr)
    return pl.pallas_call(
        functools.partial(_prep_kernel, axis=0, valid=r, blk=tr),
        grid=(g, nj),
        in_specs=[pl.BlockSpec((None, tr, n), lambda i, j: (i, j, 0))],
        out_specs=pl.BlockSpec((None, tr, n), lambda i, j: (i, j, 0)),
        out_shape=jax.ShapeDtypeStruct((g, nj * tr, n), BF16),
        compiler_params=_params(("parallel", "parallel")),
        name="bf16_row_pad",
    )(w)


def _ada_kernel(c_ref, w_ref, b_ref, o_ref):
    s = _silu(c_ref[...]).astype(BF16)
    o_ref[...] = jnp.dot(s, w_ref[...].astype(BF16), preferred_element_type=F32) + b_ref[...]


def _ada_table(cvec, w_ada, b_ada):
    depth, d, n = w_ada.shape
    tn = min(ADA_CHUNK, d)
    assert n % tn == 0
    return pl.pallas_call(
        _ada_kernel,
        grid=(depth, n // tn),
        in_specs=[
            pl.BlockSpec((MOD_ROWS, d), lambda l, j: (0, 0)),
            pl.BlockSpec((None, d, tn), lambda l, j: (l, 0, j)),
            pl.BlockSpec((None, 1, tn), lambda l, j: (l, 0, j)),
        ],
        out_specs=pl.BlockSpec((None, MOD_ROWS, tn), lambda l, j: (l, 0, j)),
        out_shape=jax.ShapeDtypeStruct((depth, MOD_ROWS, n), F32),
        compiler_params=_params(("arbitrary", "arbitrary")),
        name="ada_table",
    )(cvec, w_ada, b_ada.reshape(depth, 1, n))


def _mod_spec(layer, m, tm, d, n_prompt, dec_seq):
    def idx(i, *_):
        row0 = i * tm
        grp = jnp.where(row0 < n_prompt, 0, 1 + (row0 - n_prompt) // dec_seq)
        return (layer, grp, m, 0, 0)
    return pl.BlockSpec((None, None, None, 1, d), idx)


def _store_norm_mod(x_ref, g_ref, shift_ref, scale_ref, hn_ref):
    gain = g_ref[...] * (1.0 + scale_ref[...])
    shift = shift_ref[...]

    def body(i, carry):
        r = pl.multiple_of(i * NORM_ROWS, NORM_ROWS)
        x = x_ref[pl.ds(r, NORM_ROWS), :]
        ms = jnp.mean(x * x, axis=-1, keepdims=True)
        hn_ref[pl.ds(r, NORM_ROWS), :] = (x * lax.rsqrt(ms + EPS) * gain + shift).astype(hn_ref.dtype)
        return carry

    lax.fori_loop(0, x_ref.shape[0] // NORM_ROWS, body, 0, unroll=8)


def _ffn_kernel(*refs, n_split, n_x, prompt_tiles, final):
    x_refs, refs = refs[:n_x], refs[n_x:]
    shift_ref, scale_ref, gate_ref, g_ref, wg_ref, wu_ref, wd_ref = refs[:7]
    if final:
        fg_ref, yp_ref, ys_ref, hn_ref, acc_ref = refs[7:]
    else:
        o_ref, hn_ref = refs[7:]
        acc_ref = o_ref
    i = pl.program_id(0)
    j = pl.program_id(1)

    def per_stream(fn_prompt, fn_latent):
        pl.when(i < prompt_tiles)(fn_prompt)
        pl.when(i >= prompt_tiles)(fn_latent)

    def with_x(fn):
        if n_x == 1:
            fn(x_refs[0])
        else:
            per_stream(lambda: fn(x_refs[0]), lambda: fn(x_refs[1]))

    @pl.when(j == 0)
    def _():
        with_x(lambda x_ref: _store_norm_mod(x_ref, g_ref, shift_ref, scale_ref, hn_ref))
        acc_ref[...] = jnp.zeros_like(acc_ref)

    hn = hn_ref[...]
    cw = wg_ref.shape[-1] // n_split
    part = None
    for c in range(n_split):
        sl = slice(c * cw, (c + 1) * cw)
        a = jnp.dot(hn, wg_ref[:, sl], preferred_element_type=F32)
        u = jnp.dot(hn, wu_ref[:, sl], preferred_element_type=F32)
        h = (_silu(a) * u).astype(BF16)
        p = jnp.dot(h, wd_ref[sl, :], preferred_element_type=F32)
        part = p if part is None else part + p
    acc_ref[...] += part

    @pl.when(j == pl.num_programs(1) - 1)
    def _():
        half_gate = FFN_RES * gate_ref[...]

        def finish(x_ref):
            if not final:
                o_ref[...] = x_ref[...] + half_gate * acc_ref[...]
                return

            def normed_to(y_ref):
                def body(t, carry):
                    rs = pl.ds(pl.multiple_of(t * NORM_ROWS, NORM_ROWS), NORM_ROWS)
                    y = x_ref[rs, :] + half_gate * acc_ref[rs, :]
                    ms = jnp.mean(y * y, axis=-1, keepdims=True)
                    y_ref[rs, :] = (y * lax.rsqrt(ms + EPS)) * fg_ref[...]
                    return carry

                lax.fori_loop(0, x_ref.shape[0] // NORM_ROWS, body, 0, unroll=8)

            per_stream(lambda: normed_to(yp_ref), lambda: normed_to(ys_ref))

        with_x(finish)


def _ffn(xs, mod, norm_g, wg, wu, wd, layer, sub, mod0, n_prompt, dec_seq, final_g=None, tm=512, n_split=2):
    d = xs[0].shape[1]
    m_tok = sum(x.shape[0] for x in xs)
    n_chunks, tf = wg.shape[1], wg.shape[-1]
    ls = 2 * layer + sub
    pt = n_prompt // tm
    mspec = functools.partial(_mod_spec, layer, tm=tm, d=d, n_prompt=n_prompt, dec_seq=dec_seq)
    split_specs = [pl.BlockSpec((tm, d), lambda i, j: (jnp.minimum(i, pt - 1), 0)),
                   pl.BlockSpec((tm, d), lambda i, j: (jnp.maximum(i - pt, 0), 0))]
    whole_spec = pl.BlockSpec((tm, d), lambda i, j: (i, 0))
    in_specs = (split_specs if len(xs) == 2 else [whole_spec]) + [
        mspec(m=mod0), mspec(m=mod0 + 1), mspec(m=mod0 + 2),
        pl.BlockSpec((None, None, 1, d), lambda i, j: (layer, 2 * sub, 0, 0)),
        pl.BlockSpec((None, None, d, tf), lambda i, j: (ls, j, 0, 0)),
        pl.BlockSpec((None, None, d, tf), lambda i, j: (ls, j, 0, 0)),
        pl.BlockSpec((None, tf, d), lambda i, j: (ls, j, 0)),
    ]
    args = list(xs) + [mod, mod, mod, norm_g, wg, wu, wd]
    scratch = [pltpu.VMEM((tm, d), BF16)]
    if final_g is None:
        out_specs = whole_spec
        out_shape = jax.ShapeDtypeStruct((m_tok, d), F32)
    else:
        in_specs.append(pl.BlockSpec((1, d), lambda i, j: (0, 0)))
        args.append(final_g.reshape(1, d))
        out_specs = split_specs
        out_shape = [jax.ShapeDtypeStruct((n_prompt, d), F32), jax.ShapeDtypeStruct((m_tok - n_prompt, d), F32)]
        scratch.append(pltpu.VMEM((tm, d), F32))
    return pl.pallas_call(
        functools.partial(_ffn_kernel, n_split=n_split, n_x=len(xs), prompt_tiles=pt, final=final_g is not None),
        grid=(m_tok // tm, n_chunks),
        in_specs=in_specs,
        out_specs=out_specs,
        out_shape=out_shape,
        scratch_shapes=scratch,
        compiler_params=_params(("arbitrary" if final_g is not None else "parallel", "arbitrary")),
        name=f"ffn_l{layer}_s{sub}",
    )(*args)


def _inproj_kernel(x_ref, shift_ref, scale_ref, g_ref, w_ref, a_ref, kv_ref, gt_ref, hn_ref,
                   *, na, nkv, n_split):
    j = pl.program_id(1)

    @pl.when(j == 0)
    def _():
        _store_norm_mod(x_ref, g_ref, shift_ref, scale_ref, hn_ref)

    cw = w_ref.shape[-1] // n_split

    def project(o_ref):
        hn = hn_ref[...]
        for c in range(n_split):
            cs = slice(c * cw, (c + 1) * cw)
            o_ref[:, cs] = jnp.dot(hn, w_ref[:, cs], preferred_element_type=F32).astype(o_ref.dtype)

    @pl.when(j < na)
    def _():
        project(a_ref)

    @pl.when(jnp.logical_and(j >= na, j < na + nkv))
    def _():
        project(kv_ref)

    @pl.when(j >= na + nkv)
    def _():
        project(gt_ref)


def _inproj(x, mod, norm_g, w_in, layer, a_cols, kv_cols, g_cols, n_prompt, dec_seq, tm=1024, n_split=2):
    m_tok, d = x.shape
    n_chunks, tn = w_in.shape[1], w_in.shape[-1]
    na, nkv, ng = a_cols // tn, kv_cols // tn, g_cols // tn
    assert na * tn == a_cols and nkv * tn == kv_cols and ng * tn == g_cols and na + nkv + ng == n_chunks
    mspec = functools.partial(_mod_spec, layer, tm=tm, d=d, n_prompt=n_prompt, dec_seq=dec_seq)
    return pl.pallas_call(
        functools.partial(_inproj_kernel, na=na, nkv=nkv, n_split=n_split),
        grid=(m_tok // tm, n_chunks),
        in_specs=[
            pl.BlockSpec((tm, d), lambda i, j: (i, 0)),
            mspec(m=3), mspec(m=4),
            pl.BlockSpec((None, None, 1, d), lambda i, j: (layer, 1, 0, 0)),
            pl.BlockSpec((None, None, d, tn), lambda i, j: (layer, j, 0, 0)),
        ],
        out_specs=[
            pl.BlockSpec((tm, tn), lambda i, j: (i, jnp.minimum(j, na - 1))),
            pl.BlockSpec((tm, tn), lambda i, j: (i, jnp.clip(j - na, 0, nkv - 1))),
            pl.BlockSpec((tm, tn), lambda i, j: (i, jnp.clip(j - na - nkv, 0, ng - 1))),
        ],
        out_shape=[
            jax.ShapeDtypeStruct((m_tok, a_cols), BF16),
            jax.ShapeDtypeStruct((m_tok, kv_cols), F32),
            jax.ShapeDtypeStruct((m_tok, g_cols), BF16),
        ],
        scratch_shapes=[pltpu.VMEM((tm, d), BF16)],
        compiler_params=_params(("parallel", "arbitrary")),
        name=f"inproj_l{layer}",
    )(x, mod, mod, norm_g, w_in)


def _conv_tile(i, prev_ref, cur_ref, next_ref, w_ref, b_ref, lg_ref, lb_ref, hs_ref, sh_ref, cv_ref,
               *, tp, dc, prompt_tiles, tiles_per_seq):
    def glu(ref):
        z = ref[...].astype(F32)
        return z[:, :dc] * jax.nn.sigmoid(z[:, dc:])

    pos = lax.rem(i - prompt_tiles, tiles_per_seq)
    is_latent = i >= prompt_tiles
    has_prev = jnp.logical_and(is_latent, pos != 0)
    has_next = jnp.logical_and(is_latent, pos != tiles_per_seq - 1)
    hs_ref[0:HALO, :] = jnp.where(has_prev, glu(prev_ref), 0.0)
    hs_ref[HALO:HALO + tp, :] = glu(cur_ref)
    hs_ref[HALO + tp:2 * HALO + tp, :] = jnp.where(has_next, glu(next_ref), 0.0)

    base = HALO - CONV_WIDTH // 2
    ext = sh_ref.shape[1]
    for s in range(SUBLANES):
        sh_ref[s] = hs_ref[base + s:base + s + ext, :]

    rc = 32
    for c0 in range(0, dc, LANES):
        for r0 in range(0, tp, rc):
            acc = jnp.zeros((rc, LANES), F32)
            for k in range(CONV_WIDTH):
                st = r0 + SUBLANES * (k // SUBLANES)
                acc = acc + sh_ref[k % SUBLANES, st:st + rc, c0:c0 + LANES] * w_ref[k:k + 1, c0:c0 + LANES]
            cv_ref[r0:r0 + rc, c0:c0 + LANES] = acc

    h = cv_ref[...] + b_ref[...]
    mu = jnp.mean(h, axis=-1, keepdims=True)
    hc = h - mu
    var = jnp.mean(hc * hc, axis=-1, keepdims=True)
    y = hc * lax.rsqrt(var + EPS) * lg_ref[...] + lb_ref[...]
    return _silu(y)


def _dot_nt(a, b):
    return lax.dot_general(a, b, (((1,), (1,)), ((), ())), preferred_element_type=F32)


def _ctx_attn_kernel(q_ref, k_ref, v_ref, o_ref):
    scale = HEAD_DIM ** -0.5
    outs = []
    for h in range(N_HEADS):
        sl = slice(h * HEAD_DIM, (h + 1) * HEAD_DIM)
        q = q_ref[:, sl]
        k = k_ref[:, sl].astype(BF16)
        v = v_ref[:, sl].astype(BF16)
        s = _dot_nt(q, k) * scale
        e = jnp.exp(s - jnp.max(s, axis=-1, keepdims=True))
        p = e / jnp.sum(e, axis=-1, keepdims=True)
        outs.append(jnp.dot(p.astype(BF16), v, preferred_element_type=F32))
    o_ref[...] = jnp.concatenate(outs, axis=-1).astype(o_ref.dtype)


def _ctx_attention(a, kv, batch, seq, q_col0):
    d_na = N_HEADS * HEAD_DIM
    qb = q_col0 // d_na
    return pl.pallas_call(
        _ctx_attn_kernel,
        grid=(batch,),
        in_specs=[
            pl.BlockSpec((seq, d_na), lambda b: (b, qb)),
            pl.BlockSpec((seq, d_na), lambda b: (b, 0)),
            pl.BlockSpec((seq, d_na), lambda b: (b, 1)),
        ],
        out_specs=pl.BlockSpec((seq, d_na), lambda b: (b, 0)),
        out_shape=jax.ShapeDtypeStruct((batch * seq, d_na), BF16),
        compiler_params=_params(("parallel",)),
        name="ctx_attn",
    )(a, kv, kv)


NA_QROWS = 4
NA_KROWS = NA_QROWS + WIN_ROWS


def _na_kernel(q_ref, k_ref, v_ref, ck_ref, cv_ref, bias_ref, o_ref, *, mb, rows):
    n_blk = rows // NA_QROWS
    qn = NA_QROWS * GRID_W
    kn = NA_KROWS * GRID_W
    first = lax.broadcasted_iota(jnp.int32, (qn, 2 * HEAD_DIM), 1) < HEAD_DIM
    ck = ck_ref[...]
    cv = cv_ref[...]
    for mi in range(mb):
        m = pl.program_id(2) * mb + mi
        kr0 = jnp.clip(NA_QROWS * m - WIN_ROWS // 2, 0, rows - NA_KROWS)
        k0 = pl.multiple_of(kr0 * GRID_W, NA_QROWS * GRID_W)
        ku = k_ref[pl.ds(k0, kn), :].astype(BF16)
        vu = v_ref[pl.ds(k0, kn), :].astype(BF16)
        variant = jnp.where(m == 0, 0, jnp.where(m == n_blk - 1, 2, 1))
        q = q_ref[mi * qn:(mi + 1) * qn, :].astype(F32) * (HEAD_DIM ** -0.5)
        out = None
        for hh in range(2):
            own = first if hh == 0 else jnp.logical_not(first)
            qh = jnp.where(own, q, 0.0).astype(BF16)
            s_loc = _dot_nt(qh, ku) + bias_ref[variant, hh]
            s_ctx = _dot_nt(qh, ck)
            mx = jnp.maximum(jnp.max(s_loc, axis=-1, keepdims=True), jnp.max(s_ctx, axis=-1, keepdims=True))
            e_loc = jnp.exp(s_loc - mx)
            e_ctx = jnp.exp(s_ctx - mx)
            denom = jnp.sum(e_loc, axis=-1, keepdims=True) + jnp.sum(e_ctx, axis=-1, keepdims=True)
            o = (jnp.dot(e_loc.astype(BF16), vu, preferred_element_type=F32)
                 + jnp.dot(e_ctx.astype(BF16), cv, preferred_element_type=F32)) / denom
            out = o if hh == 0 else jnp.where(first, out, o)
        o_ref[mi * qn:(mi + 1) * qn, :] = out.astype(o_ref.dtype)


def _na_bias_table(rel_bias, rows):
    n_blk = rows // NA_QROWS
    assert rows % NA_QROWS == 0 and n_blk >= 3 and rows >= NA_KROWS + WIN_ROWS // 2
    depth, n_heads = rel_bias.shape[:2]
    n_drow = 2 * WIN_ROWS - 1
    w = jnp.arange(GRID_W)[:, None]
    j = jnp.arange(GRID_W)[None, :]
    cs = jnp.clip(w - WIN_COLS // 2, 0, GRID_W - WIN_COLS)
    vcol = (j >= cs) & (j < cs + WIN_COLS)
    dcol = jnp.clip(j - w + (WIN_COLS - 1), 0, 2 * WIN_COLS - 2)
    cols = jnp.where(vcol[:, None, :], jnp.transpose(rel_bias[..., dcol], (0, 1, 3, 2, 4)), NEG_INF)
    cols = cols.reshape(depth, n_heads, GRID_W, n_drow * GRID_W)

    def masked(n):
        return jnp.full((depth, n_heads, GRID_W, n * GRID_W), NEG_INF, F32)

    tabs = []
    for m in (0, 1, n_blk - 1):
        kr0 = min(max(NA_QROWS * m - WIN_ROWS // 2, 0), rows - NA_KROWS)
        blocks = []
        for rq in range(NA_QROWS):
            r = NA_QROWS * m + rq
            rs = min(max(r - WIN_ROWS // 2, 0), rows - WIN_ROWS)
            u0 = rs - kr0
            d0 = rs - r + (WIN_ROWS - 1)
            parts = [masked(u0), cols[..., d0 * GRID_W:(d0 + WIN_ROWS) * GRID_W],
                     masked(NA_KROWS - WIN_ROWS - u0)]
            blocks.append(jnp.concatenate([p for p in parts if p.shape[-1]], axis=-1))
        tabs.append(jnp.concatenate(blocks, axis=2))
    return jnp.stack(tabs, axis=1)


def _na_attention(a, kv, ck, cv, bias, layer, n_prompt, dec_batch, dec_seq, q_col0, mb=8):
    d_na = N_HEADS * HEAD_DIM
    rows = dec_seq // GRID_W
    hpw = 2 * HEAD_DIM
    qrows = mb * NA_QROWS * GRID_W
    steps = dec_seq // qrows
    past = ck.shape[2]
    assert n_prompt % qrows == 0 and n_prompt % dec_seq == 0 and dec_seq % qrows == 0
    kern = functools.partial(_na_kernel, mb=mb, rows=rows)
    q_blk0 = n_prompt // qrows
    kv_blk0 = n_prompt // dec_seq
    return pl.pallas_call(
        kern,
        grid=(N_HEADS // 2, dec_batch, steps),
        in_specs=[
            pl.BlockSpec((qrows, hpw), lambda hp, b, r: (q_blk0 + b * steps + r, q_col0 // hpw + hp)),
            pl.BlockSpec((dec_seq, hpw), lambda hp, b, r: (kv_blk0 + b, hp)),
            pl.BlockSpec((dec_seq, hpw), lambda hp, b, r: (kv_blk0 + b, d_na // hpw + hp)),
            pl.BlockSpec((None, None, past, hpw), lambda hp, b, r: (b, layer, 0, hp)),
            pl.BlockSpec((None, None, past, hpw), lambda hp, b, r: (b, layer, 0, hp)),
            pl.BlockSpec((None, 3, 2, NA_QROWS * GRID_W, NA_KROWS * GRID_W), lambda hp, b, r: (layer, 0, hp, 0, 0)),
        ],
        out_specs=pl.BlockSpec((qrows, hpw), lambda hp, b, r: (b * steps + r, hp)),
        out_shape=jax.ShapeDtypeStruct((dec_batch * dec_seq, d_na), BF16),
        compiler_params=_params(("parallel", "parallel", "arbitrary")),
        name=f"na_l{layer}",
    )(a, kv, kv, ck, cv, bias)


def _merge_conv_kernel(x_ref, gate_ref, nop_ref, nos_ref, gc_ref, gn_ref, wc_ref, wn_ref, wo_ref,
                       prev_ref, cur_ref, next_ref, cw_ref, cb_ref, lg_ref, lb_ref,
                       o_ref, hs_ref, sh_ref, cv_ref, ch_ref, *, n_tiles, prompt_tiles, conv_kw):
    s = pl.program_id(0)

    @pl.when(s == 0)
    def _():
        ch_ref[...] = jnp.zeros_like(ch_ref)

    ch = ch_ref[(s + 1) % 2]
    no = jnp.where(jnp.maximum(s - 1, 0) < prompt_tiles, nop_ref[...], nos_ref[...])
    yc = jnp.dot(ch, wc_ref[...], preferred_element_type=F32)
    yn = jnp.dot(no, wn_ref[...], preferred_element_type=F32)
    m = (jax.nn.sigmoid(gc_ref[...].astype(F32)) * yc + jax.nn.sigmoid(gn_ref[...].astype(F32)) * yn)
    o_ref[...] = x_ref[...] + gate_ref[...] * jnp.dot(m.astype(BF16), wo_ref[...], preferred_element_type=F32)

    ch_new = _conv_tile(jnp.minimum(s, n_tiles - 1), prev_ref, cur_ref, next_ref, cw_ref, cb_ref, lg_ref, lb_ref,
                        hs_ref, sh_ref, cv_ref, **conv_kw)
    ch_ref[s % 2] = ch_new.astype(ch_ref.dtype)


def _merge_conv(x, mod, a, na_p, na_s, gates, conv_w, conv_b, ln_g, ln_b, w_conv_out, w_na_out, w_out,
                layer, n_prompt, seq, dec_seq, tm=256):
    m_tok, d = x.shape
    dc = conv_w.shape[-1]
    dn = na_p.shape[1]
    assert seq == tm and dec_seq % tm == 0 and n_prompt % tm == 0
    n_tiles = m_tok // tm
    pt = n_prompt // tm
    hb = tm // HALO
    n_hblk = m_tok // HALO

    def mt(s):
        return jnp.maximum(s - 1, 0)

    def ct(s):
        return jnp.minimum(s, n_tiles - 1)

    def gate_idx(s):
        row0 = mt(s) * tm
        return (layer, jnp.where(row0 < n_prompt, 0, 1 + (row0 - n_prompt) // dec_seq), 5, 0, 0)

    vec = pl.BlockSpec((None, 1, dc), lambda s: (layer, 0, 0))
    conv_kw = dict(tp=tm, dc=dc, prompt_tiles=pt, tiles_per_seq=dec_seq // tm)
    return pl.pallas_call(
        functools.partial(_merge_conv_kernel, n_tiles=n_tiles, prompt_tiles=pt, conv_kw=conv_kw),
        grid=(n_tiles + 1,),
        in_specs=[
            pl.BlockSpec((tm, d), lambda s: (mt(s), 0)),
            pl.BlockSpec((None, None, None, 1, d), gate_idx),
            pl.BlockSpec((tm, dn), lambda s: (jnp.minimum(mt(s), pt - 1), 0)),
            pl.BlockSpec((tm, dn), lambda s: (jnp.maximum(mt(s) - pt, 0), 0)),
            pl.BlockSpec((tm, d), lambda s: (mt(s), 0)),
            pl.BlockSpec((tm, d), lambda s: (mt(s), 1)),
            pl.BlockSpec((None, dc, d), lambda s: (layer, 0, 0)),
            pl.BlockSpec((None, dn, d), lambda s: (layer, 0, 0)),
            pl.BlockSpec((None, d, d), lambda s: (layer, 0, 0)),
            pl.BlockSpec((HALO, 2 * dc), lambda s: (jnp.maximum(ct(s) * hb - 1, 0), 0)),
            pl.BlockSpec((tm, 2 * dc), lambda s: (ct(s), 0)),
            pl.BlockSpec((HALO, 2 * dc), lambda s: (jnp.minimum((ct(s) + 1) * hb, n_hblk - 1), 0)),
            pl.BlockSpec((None, CONV_WIDTH, dc), lambda s: (layer, 0, 0)),
            vec, vec, vec,
        ],
        out_specs=pl.BlockSpec((tm, d), lambda s: (mt(s), 0)),
        out_shape=jax.ShapeDtypeStruct((m_tok, d), F32),
        scratch_shapes=[
            pltpu.VMEM((tm + 2 * HALO, dc), F32),
            pltpu.VMEM((SUBLANES, tm + SUBLANES * ((CONV_WIDTH - 1) // SUBLANES), dc), F32),
            pltpu.VMEM((tm, dc), F32),
            pltpu.VMEM((2, tm, dc), BF16),
        ],
        compiler_params=_params(("arbitrary",)),
        name=f"merge_conv_l{layer}",
    )(x, mod, na_p, na_s, gates, gates, w_conv_out, w_na_out, w_out, a, a, a, conv_w,
      conv_b.reshape(-1, 1, dc), ln_g.reshape(-1, 1, dc), ln_b.reshape(-1, 1, dc))


def kernel(x_prompt, x_sample, cache_k, cache_v, c, c_ctx, w_ada, b_ada, norm_g, ffn_w_gate, ffn_w_up,
           ffn_w_down, w_in, conv_w, conv_b, conv_ln_g, conv_ln_b, w_conv_out, rel_bias, w_na_out, w_out,
           final_g):
    batch, seq, d = x_prompt.shape
    dec_batch, dec_seq, _ = x_sample.shape
    depth = w_ada.shape[0]
    d_ff = ffn_w_gate.shape[-1]
    d_na = N_HEADS * HEAD_DIM
    dc = conv_w.shape[-1]
    past = cache_k.shape[2]
    n_prompt = batch * seq
    n_latent = dec_batch * dec_seq

    cvec = jnp.concatenate([c_ctx[None, :], c, jnp.zeros((MOD_ROWS - 1 - dec_batch, d), F32)], axis=0)
    mod = _ada_table(cvec, w_ada, b_ada).reshape(depth, MOD_ROWS, N_MOD, 1, d)

    wg = _bf16_col_chunks(ffn_w_gate.reshape(depth * 2, d, d_ff), FFN_CHUNK)
    wu = _bf16_col_chunks(ffn_w_up.reshape(depth * 2, d, d_ff), FFN_CHUNK)
    wd = _bf16_row_pad(ffn_w_down.reshape(depth * 2, d_ff, d), FFN_CHUNK)
    w_in_b = _bf16_col_chunks(w_in, INPROJ_CHUNK)
    w_co_b = w_conv_out.astype(BF16)
    w_no_b = w_na_out.astype(BF16)
    w_o_b = w_out.astype(BF16)
    ck = cache_k.reshape(dec_batch, depth, past, d_na).astype(BF16)
    cv = cache_v.reshape(dec_batch, depth, past, d_na).astype(BF16)
    norm_g4 = norm_g.reshape(depth, 3, 1, d)

    a_cols = 2 * dc + d_na
    na_bias = _na_bias_table(rel_bias, dec_seq // GRID_W)

    xs = (x_prompt.reshape(n_prompt, d), x_sample.reshape(n_latent, d))
    keys, vals = [], []
    for l in range(depth):
        x = _ffn(xs, mod, norm_g4, wg, wu, wd, l, 0, 0, n_prompt, dec_seq)

        a, kv, gates = _inproj(x, mod, norm_g4, w_in_b, l, a_cols, 2 * d_na, 2 * d, n_prompt, dec_seq)
        keys.append(kv[:n_prompt, :d_na].reshape(batch, seq, N_HEADS, HEAD_DIM))
        vals.append(kv[:n_prompt, d_na:].reshape(batch, seq, N_HEADS, HEAD_DIM))

        na_p = _ctx_attention(a, kv, batch, seq, 2 * dc)
        na_s = _na_attention(a, kv, ck, cv, na_bias, l, n_prompt, dec_batch, dec_seq, 2 * dc)
        x = _merge_conv(x, mod, a, na_p, na_s, gates, conv_w, conv_b, conv_ln_g, conv_ln_b, w_co_b, w_no_b,
                        w_o_b, l, n_prompt, seq, dec_seq)

        out = _ffn((x,), mod, norm_g4, wg, wu, wd, l, 1, 6, n_prompt, dec_seq,
                   final_g=final_g if l == depth - 1 else None)
        xs = (out,)

    y_prompt, y_sample = out
    return (y_prompt.reshape(batch, seq, d), y_sample.reshape(dec_batch, dec_seq, d),
            jnp.stack(keys, axis=1), jnp.stack(vals, axis=1))
```

```python
import functools

import jax
import jax.numpy as jnp
from jax import lax
from jax.experimental import pallas as pl
from jax.experimental.pallas import tpu as pltpu

F32 = jnp.float32
BF16 = jnp.bfloat16

N_HEADS = 16
HEAD_DIM = 64
GRID_W = 64
WIN_ROWS = 8
WIN_COLS = 16
CONV_WIDTH = 31
N_MOD = 9
FFN_RES = 0.5
EPS = 1e-6
NEG_INF = -1e30

SUBLANES = 8
LANES = 128
FFN_CHUNK = 512
INPROJ_CHUNK = 1024
ADA_CHUNK = 1024
MOD_ROWS = 8
NORM_ROWS = 16
HALO = 16
VMEM_LIMIT = 56 * 1024 * 1024


def _params(sem, vmem=VMEM_LIMIT):
    return pltpu.CompilerParams(dimension_semantics=sem, vmem_limit_bytes=vmem)


def _silu(x):
    return x * jax.nn.sigmoid(x)


def _prep_kernel(w_ref, o_ref, *, axis, valid, blk):
    w = w_ref[...]
    pos = pl.program_id(1) * blk + lax.broadcasted_iota(jnp.int32, w.shape, axis)
    o_ref[...] = jnp.where(pos < valid, w, 0.0).astype(o_ref.dtype)


def _bf16_col_chunks(w, tn):
    g, k, n = w.shape
    nj = pl.cdiv(n, tn)
    return pl.pallas_call(
        functools.partial(_prep_kernel, axis=1, valid=n, blk=tn),
        grid=(g, nj),
        in_specs=[pl.BlockSpec((None, k, tn), lambda i, j: (i, 0, j))],
        out_specs=pl.BlockSpec((None, None, k, tn), lambda i, j: (i, j, 0, 0)),
        out_shape=jax.ShapeDtypeStruct((g, nj, k, tn), BF16),
        compiler_params=_params(("parallel", "parallel")),
        name="bf16_col_chunks",
    )(w)


def _bf16_row_pad(w, tr):
    g, r, n = w.shape
    nj = pl.cdiv(r, tr)
    return pl.pallas_call(
        functools.partial(_prep_kernel, axis=0, valid=r, blk=tr),
        grid=(g, nj),
        in_specs=[pl.BlockSpec((None, tr, n), lambda i, j: (i, j, 0))],
        out_specs=pl.BlockSpec((None, tr, n), lambda i, j: (i, j, 0)),
        out_shape=jax.ShapeDtypeStruct((g, nj * tr, n), BF16),
        compiler_params=_params(("parallel", "parallel")),
        name="bf16_row_pad",
    )(w)


def _ada_kernel(c_ref, w_ref, b_ref, o_ref):
    s = _silu(c_ref[...]).astype(BF16)
    o_ref[...] = jnp.dot(s, w_ref[...].astype(BF16), preferred_element_type=F32) + b_ref[...]


def _ada_table(cvec, w_ada, b_ada):
    depth, d, n = w_ada.shape
    tn = min(ADA_CHUNK, d)
    assert n % tn == 0
    return pl.pallas_call(
        _ada_kernel,
        grid=(depth, n // tn),
        in_specs=[
            pl.BlockSpec((MOD_ROWS, d), lambda l, j: (0, 0)),
            pl.BlockSpec((None, d, tn), lambda l, j: (l, 0, j)),
            pl.BlockSpec((None, 1, tn), lambda l, j: (l, 0, j)),
        ],
        out_specs=pl.BlockSpec((None, MOD_ROWS, tn), lambda l, j: (l, 0, j)),
        out_shape=jax.ShapeDtypeStruct((depth, MOD_ROWS, n), F32),
        compiler_params=_params(("arbitrary", "arbitrary")),
        name="ada_table",
    )(cvec, w_ada, b_ada.reshape(depth, 1, n))


def _mod_spec(layer, m, tm, d, n_prompt, dec_seq):
    def idx(i, *_):
        row0 = i * tm
        grp = jnp.where(row0 < n_prompt, 0, 1 + (row0 - n_prompt) // dec_seq)
        return (layer, grp, m, 0, 0)
    return pl.BlockSpec((None, None, None, 1, d), idx)


def _store_norm_mod(x_ref, g_ref, shift_ref, scale_ref, hn_ref):
    gain = g_ref[...] * (1.0 + scale_ref[...])
    shift = shift_ref[...]

    def body(i, carry):
        r = pl.multiple_of(i * NORM_ROWS, NORM_ROWS)
        x = x_ref[pl.ds(r, NORM_ROWS), :]
        ms = jnp.mean(x * x, axis=-1, keepdims=True)
        hn_ref[pl.ds(r, NORM_ROWS), :] = (x * lax.rsqrt(ms + EPS) * gain + shift).astype(hn_ref.dtype)
        return carry

    lax.fori_loop(0, x_ref.shape[0] // NORM_ROWS, body, 0, unroll=8)


def _ffn_kernel(*refs, n_split, n_x, prompt_tiles, final):
    x_refs, refs = refs[:n_x], refs[n_x:]
    shift_ref, scale_ref, gate_ref, g_ref, wg_ref, wu_ref, wd_ref = refs[:7]
    if final:
        fg_ref, yp_ref, ys_ref, hn_ref, acc_ref = refs[7:]
    else:
        o_ref, hn_ref = refs[7:]
        acc_ref = o_ref
    i = pl.program_id(0)
    j = pl.program_id(1)

    def per_stream(fn_prompt, fn_latent):
        pl.when(i < prompt_tiles)(fn_prompt)
        pl.when(i >= prompt_tiles)(fn_latent)

    def with_x(fn):
        if n_x == 1:
            fn(x_refs[0])
        else:
            per_stream(lambda: fn(x_refs[0]), lambda: fn(x_refs[1]))

    @pl.when(j == 0)
    def _():
        with_x(lambda x_ref: _store_norm_mod(x_ref, g_ref, shift_ref, scale_ref, hn_ref))
        acc_ref[...] = jnp.zeros_like(acc_ref)

    hn = hn_ref[...]
    cw = wg_ref.shape[-1] // n_split
    part = None
    for c in range(n_split):
        sl = slice(c * cw, (c + 1) * cw)
        a = jnp.dot(hn, wg_ref[:, sl], preferred_element_type=F32)
        u = jnp.dot(hn, wu_ref[:, sl], preferred_element_type=F32)
        h = (_silu(a) * u).astype(BF16)
        p = jnp.dot(h, wd_ref[sl, :], preferred_element_type=F32)
        part = p if part is None else part + p
    acc_ref[...] += part

    @pl.when(j == pl.num_programs(1) - 1)
    def _():
        half_gate = FFN_RES * gate_ref[...]

        def finish(x_ref):
            if not final:
                o_ref[...] = x_ref[...] + half_gate * acc_ref[...]
                return

            def normed_to(y_ref):
                def body(t, carry):
                    rs = pl.ds(pl.multiple_of(t * NORM_ROWS, NORM_ROWS), NORM_ROWS)
                    y = x_ref[rs, :] + half_gate * acc_ref[rs, :]
                    ms = jnp.mean(y * y, axis=-1, keepdims=True)
                    y_ref[rs, :] = (y * lax.rsqrt(ms + EPS)) * fg_ref[...]
                    return carry

                lax.fori_loop(0, x_ref.shape[0] // NORM_ROWS, body, 0, unroll=8)

            per_stream(lambda: normed_to(yp_ref), lambda: normed_to(ys_ref))

        with_x(finish)


def _ffn(xs, mod, norm_g, wg, wu, wd, layer, sub, mod0, n_prompt, dec_seq, final_g=None, tm=512, n_split=2):
    d = xs[0].shape[1]
    m_tok = sum(x.shape[0] for x in xs)
    n_chunks, tf = wg.shape[1], wg.shape[-1]
    ls = 2 * layer + sub
    pt = n_prompt // tm
    mspec = functools.partial(_mod_spec, layer, tm=tm, d=d, n_prompt=n_prompt, dec_seq=dec_seq)
    split_specs = [pl.BlockSpec((tm, d), lambda i, j: (jnp.minimum(i, pt - 1), 0)),
                   pl.BlockSpec((tm, d), lambda i, j: (jnp.maximum(i - pt, 0), 0))]
    whole_spec = pl.BlockSpec((tm, d), lambda i, j: (i, 0))
    in_specs = (split_specs if len(xs) == 2 else [whole_spec]) + [
        mspec(m=mod0), mspec(m=mod0 + 1), mspec(m=mod0 + 2),
        pl.BlockSpec((None, None, 1, d), lambda i, j: (layer, 2 * sub, 0, 0)),
        pl.BlockSpec((None, None, d, tf), lambda i, j: (ls, j, 0, 0)),
        pl.BlockSpec((None, None, d, tf), lambda i, j: (ls, j, 0, 0)),
        pl.BlockSpec((None, tf, d), lambda i, j: (ls, j, 0)),
    ]
    args = list(xs) + [mod, mod, mod, norm_g, wg, wu, wd]
    scratch = [pltpu.VMEM((tm, d), BF16)]
    if final_g is None:
        out_specs = whole_spec
        out_shape = jax.ShapeDtypeStruct((m_tok, d), F32)
    else:
        in_specs.append(pl.BlockSpec((1, d), lambda i, j: (0, 0)))
        args.append(final_g.reshape(1, d))
        out_specs = split_specs
        out_shape = [jax.ShapeDtypeStruct((n_prompt, d), F32), jax.ShapeDtypeStruct((m_tok - n_prompt, d), F32)]
        scratch.append(pltpu.VMEM((tm, d), F32))
    return pl.pallas_call(
        functools.partial(_ffn_kernel, n_split=n_split, n_x=len(xs), prompt_tiles=pt, final=final_g is not None),
        grid=(m_tok // tm, n_chunks),
        in_specs=in_specs,
        out_specs=out_specs,
        out_shape=out_shape,
        scratch_shapes=scratch,
        compiler_params=_params(("arbitrary" if final_g is not None else "parallel", "arbitrary")),
        name=f"ffn_l{layer}_s{sub}",
    )(*args)


def _inproj_kernel(x_ref, shift_ref, scale_ref, g_ref, w_ref, a_ref, kv_ref, gt_ref, hn_ref,
                   *, na, nkv, n_split):
    j = pl.program_id(1)

    @pl.when(j == 0)
    def _():
        _store_norm_mod(x_ref, g_ref, shift_ref, scale_ref, hn_ref)

    cw = w_ref.shape[-1] // n_split

    def project(o_ref):
        hn = hn_ref[...]
        for c in range(n_split):
            cs = slice(c * cw, (c + 1) * cw)
            o_ref[:, cs] = jnp.dot(hn, w_ref[:, cs], preferred_element_type=F32).astype(o_ref.dtype)

    @pl.when(j < na)
    def _():
        project(a_ref)

    @pl.when(jnp.logical_and(j >= na, j < na + nkv))
    def _():
        project(kv_ref)

    @pl.when(j >= na + nkv)
    def _():
        project(gt_ref)


def _inproj(x, mod, norm_g, w_in, layer, a_cols, kv_cols, g_cols, n_prompt, dec_seq, tm=1024, n_split=2):
    m_tok, d = x.shape
    n_chunks, tn = w_in.shape[1], w_in.shape[-1]
    na, nkv, ng = a_cols // tn, kv_cols // tn, g_cols // tn
    assert na * tn == a_cols and nkv * tn == kv_cols and ng * tn == g_cols and na + nkv + ng == n_chunks
    mspec = functools.partial(_mod_spec, layer, tm=tm, d=d, n_prompt=n_prompt, dec_seq=dec_seq)
    return pl.pallas_call(
        functools.partial(_inproj_kernel, na=na, nkv=nkv, n_split=n_split),
        grid=(m_tok // tm, n_chunks),
        in_specs=[
            pl.BlockSpec((tm, d), lambda i, j: (i, 0)),
            mspec(m=3), mspec(m=4),
            pl.BlockSpec((None, None, 1, d), lambda i, j: (layer, 1, 0, 0)),
            pl.BlockSpec((None, None, d, tn), lambda i, j: (layer, j, 0, 0)),
        ],
        out_specs=[
            pl.BlockSpec((tm, tn), lambda i, j: (i, jnp.minimum(j, na - 1))),
            pl.BlockSpec((tm, tn), lambda i, j: (i, jnp.clip(j - na, 0, nkv - 1))),
            pl.BlockSpec((tm, tn), lambda i, j: (i, jnp.clip(j - na - nkv, 0, ng - 1))),
        ],
        out_shape=[
            jax.ShapeDtypeStruct((m_tok, a_cols), BF16),
            jax.ShapeDtypeStruct((m_tok, kv_cols), F32),
            jax.ShapeDtypeStruct((m_tok, g_cols), BF16),
        ],
        scratch_shapes=[pltpu.VMEM((tm, d), BF16)],
        compiler_params=_params(("parallel", "arbitrary")),
        name=f"inproj_l{layer}",
    )(x, mod, mod, norm_g, w_in)


def _conv_tile(i, prev_ref, cur_ref, next_ref, w_ref, b_ref, lg_ref, lb_ref, hs_ref, sh_ref, cv_ref,
               *, tp, dc, prompt_tiles, tiles_per_seq):
    def glu(ref):
        z = ref[...].astype(F32)
        return z[:, :dc] * jax.nn.sigmoid(z[:, dc:])

    pos = lax.rem(i - prompt_tiles, tiles_per_seq)
    is_latent = i >= prompt_tiles
    has_prev = jnp.logical_and(is_latent, pos != 0)
    has_next = jnp.logical_and(is_latent, pos != tiles_per_seq - 1)
    hs_ref[0:HALO, :] = jnp.where(has_prev, glu(prev_ref), 0.0)
    hs_ref[HALO:HALO + tp, :] = glu(cur_ref)
    hs_ref[HALO + tp:2 * HALO + tp, :] = jnp.where(has_next, glu(next_ref), 0.0)

    base = HALO - CONV_WIDTH // 2
    ext = sh_ref.shape[1]
    for s in range(SUBLANES):
        sh_ref[s] = hs_ref[base + s:base + s + ext, :]

    rc = 32
    for c0 in range(0, dc, LANES):
        for r0 in range(0, tp, rc):
            acc = jnp.zeros((rc, LANES), F32)
            for k in range(CONV_WIDTH):
                st = r0 + SUBLANES * (k // SUBLANES)
                acc = acc + sh_ref[k % SUBLANES, st:st + rc, c0:c0 + LANES] * w_ref[k:k + 1, c0:c0 + LANES]
            cv_ref[r0:r0 + rc, c0:c0 + LANES] = acc

    h = cv_ref[...] + b_ref[...]
    mu = jnp.mean(h, axis=-1, keepdims=True)
    hc = h - mu
    var = jnp.mean(hc * hc, axis=-1, keepdims=True)
    y = hc * lax.rsqrt(var + EPS) * lg_ref[...] + lb_ref[...]
    return _silu(y)


def _dot_nt(a, b):
    return lax.dot_general(a, b, (((1,), (1,)), ((), ())), preferred_element_type=F32)


def _ctx_attn_kernel(q_ref, k_ref, v_ref, o_ref):
    scale = HEAD_DIM ** -0.5
    outs = []
    for h in range(N_HEADS):
        sl = slice(h * HEAD_DIM, (h + 1) * HEAD_DIM)
        q = q_ref[:, sl]
        k = k_ref[:, sl].astype(BF16)
        v = v_ref[:, sl].astype(BF16)
        s = _dot_nt(q, k) * scale
        e = jnp.exp(s - jnp.max(s, axis=-1, keepdims=True))
        p = e / jnp.sum(e, axis=-1, keepdims=True)
        outs.append(jnp.dot(p.astype(BF16), v, preferred_element_type=F32))
    o_ref[...] = jnp.concatenate(outs, axis=-1).astype(o_ref.dtype)


def _ctx_attention(a, kv, batch, seq, q_col0):
    d_na = N_HEADS * HEAD_DIM
    qb = q_col0 // d_na
    return pl.pallas_call(
        _ctx_attn_kernel,
        grid=(batch,),
        in_specs=[
            pl.BlockSpec((seq, d_na), lambda b: (b, qb)),
            pl.BlockSpec((seq, d_na), lambda b: (b, 0)),
            pl.BlockSpec((seq, d_na), lambda b: (b, 1)),
        ],
        out_specs=pl.BlockSpec((seq, d_na), lambda b: (b, 0)),
        out_shape=jax.ShapeDtypeStruct((batch * seq, d_na), BF16),
        compiler_params=_params(("parallel",)),
        name="ctx_attn",
    )(a, kv, kv)


NA_QROWS = 4
NA_KROWS = NA_QROWS + WIN_ROWS


def _na_kernel(q_ref, k_ref, v_ref, ck_ref, cv_ref, bias_ref, o_ref, *, mb, rows):
    n_blk = rows // NA_QROWS
    qn = NA_QROWS * GRID_W
    kn = NA_KROWS * GRID_W
    first = lax.broadcasted_iota(jnp.int32, (qn, 2 * HEAD_DIM), 1) < HEAD_DIM
    ck = ck_ref[...]
    cv = cv_ref[...]
    for mi in range(mb):
        m = pl.program_id(2) * mb + mi
        kr0 = jnp.clip(NA_QROWS * m - WIN_ROWS // 2, 0, rows - NA_KROWS)
        k0 = pl.multiple_of(kr0 * GRID_W, NA_QROWS * GRID_W)
        ku = k_ref[pl.ds(k0, kn), :].astype(BF16)
        vu = v_ref[pl.ds(k0, kn), :].astype(BF16)
        variant = jnp.where(m == 0, 0, jnp.where(m == n_blk - 1, 2, 1))
        q = q_ref[mi * qn:(mi + 1) * qn, :].astype(F32) * (HEAD_DIM ** -0.5)
        out = None
        for hh in range(2):
            own = first if hh == 0 else jnp.logical_not(first)
            qh = jnp.where(own, q, 0.0).astype(BF16)
            s_loc = _dot_nt(qh, ku) + bias_ref[variant, hh]
            s_ctx = _dot_nt(qh, ck)
            mx = jnp.maximum(jnp.max(s_loc, axis=-1, keepdims=True), jnp.max(s_ctx, axis=-1, keepdims=True))
            e_loc = jnp.exp(s_loc - mx)
            e_ctx = jnp.exp(s_ctx - mx)
            denom = jnp.sum(e_loc, axis=-1, keepdims=True) + jnp.sum(e_ctx, axis=-1, keepdims=True)
            o = (jnp.dot(e_loc.astype(BF16), vu, preferred_element_type=F32)
                 + jnp.dot(e_ctx.astype(BF16), cv, preferred_element_type=F32)) / denom
            out = o if hh == 0 else jnp.where(first, out, o)
        o_ref[mi * qn:(mi + 1) * qn, :] = out.astype(o_ref.dtype)


def _na_bias_table(rel_bias, rows):
    n_blk = rows // NA_QROWS
    assert rows % NA_QROWS == 0 and n_blk >= 3 and rows >= NA_KROWS + WIN_ROWS // 2
    depth, n_heads = rel_bias.shape[:2]
    n_drow = 2 * WIN_ROWS - 1
    w = jnp.arange(GRID_W)[:, None]
    j = jnp.arange(GRID_W)[None, :]
    cs = jnp.clip(w - WIN_COLS // 2, 0, GRID_W - WIN_COLS)
    vcol = (j >= cs) & (j < cs + WIN_COLS)
    off = GRID_W - WIN_COLS
    padded = jnp.pad(rel_bias, ((0, 0), (0, 0), (0, 0), (off, off)))
    cols = jnp.stack([padded[..., off + (WIN_COLS - 1) - q:off + (WIN_COLS - 1) - q + GRID_W]
                      for q in range(GRID_W)], axis=2)
    cols = jnp.where(vcol[:, None, :], cols, NEG_INF).reshape(depth, n_heads, GRID_W, n_drow * GRID_W)

    def masked(n):
        return jnp.full((depth, n_heads, GRID_W, n * GRID_W), NEG_INF, F32)

    tabs = []
    for m in (0, 1, n_blk - 1):
        kr0 = min(max(NA_QROWS * m - WIN_ROWS // 2, 0), rows - NA_KROWS)
        blocks = []
        for rq in range(NA_QROWS):
            r = NA_QROWS * m + rq
            rs = min(max(r - WIN_ROWS // 2, 0), rows - WIN_ROWS)
            u0 = rs - kr0
            d0 = rs - r + (WIN_ROWS - 1)
            parts = [masked(u0), cols[..., d0 * GRID_W:(d0 + WIN_ROWS) * GRID_W],
                     masked(NA_KROWS - WIN_ROWS - u0)]
            blocks.append(jnp.concatenate([p for p in parts if p.shape[-1]], axis=-1))
        tabs.append(jnp.concatenate(blocks, axis=2))
    return jnp.stack(tabs, axis=1)


def _na_attention(a, kv, ck, cv, bias, layer, n_prompt, dec_batch, dec_seq, q_col0, mb=8):
    d_na = N_HEADS * HEAD_DIM
    rows = dec_seq // GRID_W
    hpw = 2 * HEAD_DIM
    qrows = mb * NA_QROWS * GRID_W
    steps = dec_seq // qrows
    past = ck.shape[2]
    assert n_prompt % qrows == 0 and n_prompt % dec_seq == 0 and dec_seq % qrows == 0
    kern = functools.partial(_na_kernel, mb=mb, rows=rows)
    q_blk0 = n_prompt // qrows
    kv_blk0 = n_prompt // dec_seq
    return pl.pallas_call(
        kern,
        grid=(N_HEADS // 2, dec_batch, steps),
        in_specs=[
            pl.BlockSpec((qrows, hpw), lambda hp, b, r: (q_blk0 + b * steps + r, q_col0 // hpw + hp)),
            pl.BlockSpec((dec_seq, hpw), lambda hp, b, r: (kv_blk0 + b, hp)),
            pl.BlockSpec((dec_seq, hpw), lambda hp, b, r: (kv_blk0 + b, d_na // hpw + hp)),
            pl.BlockSpec((None, None, past, hpw), lambda hp, b, r: (b, layer, 0, hp)),
            pl.BlockSpec((None, None, past, hpw), lambda hp, b, r: (b, layer, 0, hp)),
            pl.BlockSpec((None, 3, 2, NA_QROWS * GRID_W, NA_KROWS * GRID_W), lambda hp, b, r: (layer, 0, hp, 0, 0)),
        ],
        out_specs=pl.BlockSpec((qrows, hpw), lambda hp, b, r: (b * steps + r, hp)),
        out_shape=jax.ShapeDtypeStruct((dec_batch * dec_seq, d_na), BF16),
        compiler_params=_params(("parallel", "parallel", "arbitrary")),
        name=f"na_l{layer}",
    )(a, kv, kv, ck, cv, bias)


def _merge_conv_kernel(x_ref, gate_ref, nop_ref, nos_ref, gc_ref, gn_ref, wc_ref, wn_ref, wo_ref,
                       prev_ref, cur_ref, next_ref, cw_ref, cb_ref, lg_ref, lb_ref,
                       o_ref, hs_ref, sh_ref, cv_ref, ch_ref, *, n_tiles, prompt_tiles, conv_kw):
    s = pl.program_id(0)

    @pl.when(s == 0)
    def _():
        ch_ref[...] = jnp.zeros_like(ch_ref)

    ch = ch_ref[(s + 1) % 2]
    no = jnp.where(jnp.maximum(s - 1, 0) < prompt_tiles, nop_ref[...], nos_ref[...])
    yc = jnp.dot(ch, wc_ref[...], preferred_element_type=F32)
    yn = jnp.dot(no, wn_ref[...], preferred_element_type=F32)
    m = (jax.nn.sigmoid(gc_ref[...].astype(F32)) * yc + jax.nn.sigmoid(gn_ref[...].astype(F32)) * yn)
    o_ref[...] = x_ref[...] + gate_ref[...] * jnp.dot(m.astype(BF16), wo_ref[...], preferred_element_type=F32)

    ch_new = _conv_tile(jnp.minimum(s, n_tiles - 1), prev_ref, cur_ref, next_ref, cw_ref, cb_ref, lg_ref, lb_ref,
                        hs_ref, sh_ref, cv_ref, **conv_kw)
    ch_ref[s % 2] = ch_new.astype(ch_ref.dtype)


def _merge_conv(x, mod, a, na_p, na_s, gates, conv_w, conv_b, ln_g, ln_b, w_conv_out, w_na_out, w_out,
                layer, n_prompt, seq, dec_seq, tm=256):
    m_tok, d = x.shape
    dc = conv_w.shape[-1]
    dn = na_p.shape[1]
    assert seq == tm and dec_seq % tm == 0 and n_prompt % tm == 0
    n_tiles = m_tok // tm
    pt = n_prompt // tm
    hb = tm // HALO
    n_hblk = m_tok // HALO

    def mt(s):
        return jnp.maximum(s - 1, 0)

    def ct(s):
        return jnp.minimum(s, n_tiles - 1)

    def gate_idx(s):
        row0 = mt(s) * tm
        return (layer, jnp.where(row0 < n_prompt, 0, 1 + (row0 - n_prompt) // dec_seq), 5, 0, 0)

    vec = pl.BlockSpec((None, 1, dc), lambda s: (layer, 0, 0))
    conv_kw = dict(tp=tm, dc=dc, prompt_tiles=pt, tiles_per_seq=dec_seq // tm)
    return pl.pallas_call(
        functools.partial(_merge_conv_kernel, n_tiles=n_tiles, prompt_tiles=pt, conv_kw=conv_kw),
        grid=(n_tiles + 1,),
        in_specs=[
            pl.BlockSpec((tm, d), lambda s: (mt(s), 0)),
            pl.BlockSpec((None, None, None, 1, d), gate_idx),
            pl.BlockSpec((tm, dn), lambda s: (jnp.minimum(mt(s), pt - 1), 0)),
            pl.BlockSpec((tm, dn), lambda s: (jnp.maximum(mt(s) - pt, 0), 0)),
            pl.BlockSpec((tm, d), lambda s: (mt(s), 0)),
            pl.BlockSpec((tm, d), lambda s: (mt(s), 1)),
            pl.BlockSpec((None, dc, d), lambda s: (layer, 0, 0)),
            pl.BlockSpec((None, dn, d), lambda s: (layer, 0, 0)),
            pl.BlockSpec((None, d, d), lambda s: (layer, 0, 0)),
            pl.BlockSpec((HALO, 2 * dc), lambda s: (jnp.maximum(ct(s) * hb - 1, 0), 0)),
            pl.BlockSpec((tm, 2 * dc), lambda s: (ct(s), 0)),
            pl.BlockSpec((HALO, 2 * dc), lambda s: (jnp.minimum((ct(s) + 1) * hb, n_hblk - 1), 0)),
            pl.BlockSpec((None, CONV_WIDTH, dc), lambda s: (layer, 0, 0)),
            vec, vec, vec,
        ],
        out_specs=pl.BlockSpec((tm, d), lambda s: (mt(s), 0)),
        out_shape=jax.ShapeDtypeStruct((m_tok, d), F32),
        scratch_shapes=[
            pltpu.VMEM((tm + 2 * HALO, dc), F32),
            pltpu.VMEM((SUBLANES, tm + SUBLANES * ((CONV_WIDTH - 1) // SUBLANES), dc), F32),
            pltpu.VMEM((tm, dc), F32),
            pltpu.VMEM((2, tm, dc), BF16),
        ],
        compiler_params=_params(("arbitrary",)),
        name=f"merge_conv_l{layer}",
    )(x, mod, na_p, na_s, gates, gates, w_conv_out, w_na_out, w_out, a, a, a, conv_w,
      conv_b.reshape(-1, 1, dc), ln_g.reshape(-1, 1, dc), ln_b.reshape(-1, 1, dc))


def kernel(x_prompt, x_sample, cache_k, cache_v, c, c_ctx, w_ada, b_ada, norm_g, ffn_w_gate, ffn_w_up,
           ffn_w_down, w_in, conv_w, conv_b, conv_ln_g, conv_ln_b, w_conv_out, rel_bias, w_na_out, w_out,
           final_g):
    batch, seq, d = x_prompt.shape
    dec_batch, dec_seq, _ = x_sample.shape
    depth = w_ada.shape[0]
    d_ff = ffn_w_gate.shape[-1]
    d_na = N_HEADS * HEAD_DIM
    dc = conv_w.shape[-1]
    past = cache_k.shape[2]
    n_prompt = batch * seq
    n_latent = dec_batch * dec_seq

    cvec = jnp.concatenate([c_ctx[None, :], c, jnp.zeros((MOD_ROWS - 1 - dec_batch, d), F32)], axis=0)
    mod = _ada_table(cvec, w_ada, b_ada).reshape(depth, MOD_ROWS, N_MOD, 1, d)

    wg = _bf16_col_chunks(ffn_w_gate.reshape(depth * 2, d, d_ff), FFN_CHUNK)
    wu = _bf16_col_chunks(ffn_w_up.reshape(depth * 2, d, d_ff), FFN_CHUNK)
    wd = _bf16_row_pad(ffn_w_down.reshape(depth * 2, d_ff, d), FFN_CHUNK)
    w_in_b = _bf16_col_chunks(w_in, INPROJ_CHUNK)
    w_co_b = w_conv_out.astype(BF16)
    w_no_b = w_na_out.astype(BF16)
    w_o_b = w_out.astype(BF16)
    ck = cache_k.reshape(dec_batch, depth, past, d_na).astype(BF16)
    cv = cache_v.reshape(dec_batch, depth, past, d_na).astype(BF16)
    norm_g4 = norm_g.reshape(depth, 3, 1, d)

    a_cols = 2 * dc + d_na
    na_bias = _na_bias_table(rel_bias, dec_seq // GRID_W)

    xs = (x_prompt.reshape(n_prompt, d), x_sample.reshape(n_latent, d))
    keys, vals = [], []
    for l in range(depth):
        x = _ffn(xs, mod, norm_g4, wg, wu, wd, l, 0, 0, n_prompt, dec_seq)

        a, kv, gates = _inproj(x, mod, norm_g4, w_in_b, l, a_cols, 2 * d_na, 2 * d, n_prompt, dec_seq)
        keys.append(kv[:n_prompt, :d_na].reshape(batch, seq, N_HEADS, HEAD_DIM))
        vals.append(kv[:n_prompt, d_na:].reshape(batch, seq, N_HEADS, HEAD_DIM))

        na_p = _ctx_attention(a, kv, batch, seq, 2 * dc)
        na_s = _na_attention(a, kv, ck, cv, na_bias, l, n_prompt, dec_batch, dec_seq, 2 * dc)
        x = _merge_conv(x, mod, a, na_p, na_s, gates, conv_w, conv_b, conv_ln_g, conv_ln_b, w_co_b, w_no_b,
                        w_o_b, l, n_prompt, seq, dec_seq)

        out = _ffn((x,), mod, norm_g4, wg, wu, wd, l, 1, 6, n_prompt, dec_seq,
                   final_g=final_g if l == depth - 1 else None)
        xs = (out,)

    y_prompt, y_sample = out
    return (y_prompt.reshape(batch, seq, d), y_sample.reshape(dec_batch, dec_seq, d),
            jnp.stack(keys, axis=1), jnp.stack(vals, axis=1))
```

```python
import functools

import jax
import jax.numpy as jnp
from jax import lax
from jax.experimental import pallas as pl
from jax.experimental.pallas import tpu as pltpu

F32 = jnp.float32
BF16 = jnp.bfloat16

N_HEADS = 16
HEAD_DIM = 64
GRID_W = 64
WIN_ROWS = 8
WIN_COLS = 16
CONV_WIDTH = 31
N_MOD = 9
FFN_RES = 0.5
EPS = 1e-6
NEG_INF = -1e30

SUBLANES = 8
LANES = 128
FFN_CHUNK = 512
INPROJ_CHUNK = 1024
ADA_CHUNK = 1024
MOD_ROWS = 8
NORM_ROWS = 16
NORM_UNROLL = 8
HALO = 16
VMEM_LIMIT = 56 * 1024 * 1024


def _params(sem, vmem=VMEM_LIMIT):
    return pltpu.CompilerParams(dimension_semantics=sem, vmem_limit_bytes=vmem)


def _silu(x):
    return x * jax.nn.sigmoid(x)


def _prep_kernel(w_ref, o_ref, *, axis, valid, blk):
    w = w_ref[...]
    pos = pl.program_id(1) * blk + lax.broadcasted_iota(jnp.int32, w.shape, axis)
    o_ref[...] = jnp.where(pos < valid, w, 0.0).astype(o_ref.dtype)


def _bf16_col_chunks(w, tn):
    g, k, n = w.shape
    nj = pl.cdiv(n, tn)
    return pl.pallas_call(
        functools.partial(_prep_kernel, axis=1, valid=n, blk=tn),
        grid=(g, nj),
        in_specs=[pl.BlockSpec((None, k, tn), lambda i, j: (i, 0, j))],
        out_specs=pl.BlockSpec((None, None, k, tn), lambda i, j: (i, j, 0, 0)),
        out_shape=jax.ShapeDtypeStruct((g, nj, k, tn), BF16),
        compiler_params=_params(("parallel", "parallel")),
        name="bf16_col_chunks",
    )(w)


def _bf16_row_pad(w, tr):
    g, r, n = w.shape
    nj = pl.cdiv(r, tr)
    return pl.pallas_call(
        functools.partial(_prep_kernel, axis=0, valid=r, blk=tr),
        grid=(g, nj),
        in_specs=[pl.BlockSpec((None, tr, n), lambda i, j: (i, j, 0))],
        out_specs=pl.BlockSpec((None, tr, n), lambda i, j: (i, j, 0)),
        out_shape=jax.ShapeDtypeStruct((g, nj * tr, n), BF16),
        compiler_params=_params(("parallel", "parallel")),
        name="bf16_row_pad",
    )(w)


def _ada_kernel(c_ref, w_ref, b_ref, o_ref):
    s = _silu(c_ref[...]).astype(BF16)
    o_ref[...] = jnp.dot(s, w_ref[...].astype(BF16), preferred_element_type=F32) + b_ref[...]


def _ada_table(cvec, w_ada, b_ada):
    depth, d, n = w_ada.shape
    tn = min(ADA_CHUNK, d)
    assert n % tn == 0
    return pl.pallas_call(
        _ada_kernel,
        grid=(depth, n // tn),
        in_specs=[
            pl.BlockSpec((MOD_ROWS, d), lambda l, j: (0, 0)),
            pl.BlockSpec((None, d, tn), lambda l, j: (l, 0, j)),
            pl.BlockSpec((None, 1, tn), lambda l, j: (l, 0, j)),
        ],
        out_specs=pl.BlockSpec((None, MOD_ROWS, tn), lambda l, j: (l, 0, j)),
        out_shape=jax.ShapeDtypeStruct((depth, MOD_ROWS, n), F32),
        compiler_params=_params(("arbitrary", "arbitrary")),
        name="ada_table",
    )(cvec, w_ada, b_ada.reshape(depth, 1, n))


def _mod_spec(layer, m, tm, d, n_prompt, dec_seq):
    def idx(i, *_):
        row0 = i * tm
        grp = jnp.where(row0 < n_prompt, 0, 1 + (row0 - n_prompt) // dec_seq)
        return (layer, grp, m, 0, 0)
    return pl.BlockSpec((None, None, None, 1, d), idx)


def _store_norm_mod(x_ref, g_ref, shift_ref, scale_ref, hn_ref):
    gain = g_ref[...] * (1.0 + scale_ref[...])
    shift = shift_ref[...]

    def body(i, carry):
        r = pl.multiple_of(i * NORM_ROWS, NORM_ROWS)
        x = x_ref[pl.ds(r, NORM_ROWS), :]
        ms = jnp.mean(x * x, axis=-1, keepdims=True)
        hn_ref[pl.ds(r, NORM_ROWS), :] = (x * lax.rsqrt(ms + EPS) * gain + shift).astype(hn_ref.dtype)
        return carry

    lax.fori_loop(0, x_ref.shape[0] // NORM_ROWS, body, 0, unroll=NORM_UNROLL)


def _ffn_kernel(*refs, n_split, n_x, prompt_tiles, final):
    x_refs, refs = refs[:n_x], refs[n_x:]
    shift_ref, scale_ref, gate_ref, g_ref, wg_ref, wu_ref, wd_ref = refs[:7]
    if final:
        fg_ref, yp_ref, ys_ref, hn_ref, acc_ref = refs[7:]
    else:
        o_ref, hn_ref = refs[7:]
        acc_ref = o_ref
    i = pl.program_id(0)
    j = pl.program_id(1)

    def per_stream(fn_prompt, fn_latent):
        pl.when(i < prompt_tiles)(fn_prompt)
        pl.when(i >= prompt_tiles)(fn_latent)

    def with_x(fn):
        if n_x == 1:
            fn(x_refs[0])
        else:
            per_stream(lambda: fn(x_refs[0]), lambda: fn(x_refs[1]))

    @pl.when(j == 0)
    def _():
        with_x(lambda x_ref: _store_norm_mod(x_ref, g_ref, shift_ref, scale_ref, hn_ref))
        acc_ref[...] = jnp.zeros_like(acc_ref)

    hn = hn_ref[...]
    cw = wg_ref.shape[-1] // n_split
    part = None
    for c in range(n_split):
        sl = slice(c * cw, (c + 1) * cw)
        a = jnp.dot(hn, wg_ref[:, sl], preferred_element_type=F32)
        u = jnp.dot(hn, wu_ref[:, sl], preferred_element_type=F32)
        h = (_silu(a) * u).astype(BF16)
        p = jnp.dot(h, wd_ref[sl, :], preferred_element_type=F32)
        part = p if part is None else part + p
    acc_ref[...] += part

    @pl.when(j == pl.num_programs(1) - 1)
    def _():
        half_gate = FFN_RES * gate_ref[...]

        def finish(x_ref):
            if not final:
                o_ref[...] = x_ref[...] + half_gate * acc_ref[...]
                return

            def normed_to(y_ref):
                def body(t, carry):
                    rs = pl.ds(pl.multiple_of(t * NORM_ROWS, NORM_ROWS), NORM_ROWS)
                    y = x_ref[rs, :] + half_gate * acc_ref[rs, :]
                    ms = jnp.mean(y * y, axis=-1, keepdims=True)
                    y_ref[rs, :] = (y * lax.rsqrt(ms + EPS)) * fg_ref[...]
                    return carry

                lax.fori_loop(0, x_ref.shape[0] // NORM_ROWS, body, 0, unroll=NORM_UNROLL)

            per_stream(lambda: normed_to(yp_ref), lambda: normed_to(ys_ref))

        with_x(finish)


def _ffn(xs, mod, norm_g, wg, wu, wd, layer, sub, mod0, n_prompt, dec_seq, final_g=None, tm=512, n_split=2):
    d = xs[0].shape[1]
    m_tok = sum(x.shape[0] for x in xs)
    n_chunks, tf = wg.shape[1], wg.shape[-1]
    ls = 2 * layer + sub
    pt = n_prompt // tm
    mspec = functools.partial(_mod_spec, layer, tm=tm, d=d, n_prompt=n_prompt, dec_seq=dec_seq)
    split_specs = [pl.BlockSpec((tm, d), lambda i, j: (jnp.minimum(i, pt - 1), 0)),
                   pl.BlockSpec((tm, d), lambda i, j: (jnp.maximum(i - pt, 0), 0))]
    whole_spec = pl.BlockSpec((tm, d), lambda i, j: (i, 0))
    in_specs = (split_specs if len(xs) == 2 else [whole_spec]) + [
        mspec(m=mod0), mspec(m=mod0 + 1), mspec(m=mod0 + 2),
        pl.BlockSpec((None, None, 1, d), lambda i, j: (layer, 2 * sub, 0, 0)),
        pl.BlockSpec((None, None, d, tf), lambda i, j: (ls, j, 0, 0)),
        pl.BlockSpec((None, None, d, tf), lambda i, j: (ls, j, 0, 0)),
        pl.BlockSpec((None, tf, d), lambda i, j: (ls, j, 0)),
    ]
    args = list(xs) + [mod, mod, mod, norm_g, wg, wu, wd]
    scratch = [pltpu.VMEM((tm, d), BF16)]
    if final_g is None:
        out_specs = whole_spec
        out_shape = jax.ShapeDtypeStruct((m_tok, d), F32)
    else:
        in_specs.append(pl.BlockSpec((1, d), lambda i, j: (0, 0)))
        args.append(final_g.reshape(1, d))
        out_specs = split_specs
        out_shape = [jax.ShapeDtypeStruct((n_prompt, d), F32), jax.ShapeDtypeStruct((m_tok - n_prompt, d), F32)]
        scratch.append(pltpu.VMEM((tm, d), F32))
    return pl.pallas_call(
        functools.partial(_ffn_kernel, n_split=n_split, n_x=len(xs), prompt_tiles=pt, final=final_g is not None),
        grid=(m_tok // tm, n_chunks),
        in_specs=in_specs,
        out_specs=out_specs,
        out_shape=out_shape,
        scratch_shapes=scratch,
        compiler_params=_params(("arbitrary" if final_g is not None else "parallel", "arbitrary")),
        name=f"ffn_l{layer}_s{sub}",
    )(*args)


def _inproj_kernel(x_ref, shift_ref, scale_ref, g_ref, w_ref, a_ref, kv_ref, gt_ref, hn_ref,
                   *, na, nkv, n_split):
    j = pl.program_id(1)

    @pl.when(j == 0)
    def _():
        _store_norm_mod(x_ref, g_ref, shift_ref, scale_ref, hn_ref)

    cw = w_ref.shape[-1] // n_split

    def project(o_ref):
        hn = hn_ref[...]
        for c in range(n_split):
            cs = slice(c * cw, (c + 1) * cw)
            o_ref[:, cs] = jnp.dot(hn, w_ref[:, cs], preferred_element_type=F32).astype(o_ref.dtype)

    @pl.when(j < na)
    def _():
        project(a_ref)

    @pl.when(jnp.logical_and(j >= na, j < na + nkv))
    def _():
        project(kv_ref)

    @pl.when(j >= na + nkv)
    def _():
        project(gt_ref)


def _inproj(x, mod, norm_g, w_in, layer, a_cols, kv_cols, g_cols, n_prompt, dec_seq, tm=1024, n_split=2):
    m_tok, d = x.shape
    n_chunks, tn = w_in.shape[1], w_in.shape[-1]
    na, nkv, ng = a_cols // tn, kv_cols // tn, g_cols // tn
    assert na * tn == a_cols and nkv * tn == kv_cols and ng * tn == g_cols and na + nkv + ng == n_chunks
    mspec = functools.partial(_mod_spec, layer, tm=tm, d=d, n_prompt=n_prompt, dec_seq=dec_seq)
    return pl.pallas_call(
        functools.partial(_inproj_kernel, na=na, nkv=nkv, n_split=n_split),
        grid=(m_tok // tm, n_chunks),
        in_specs=[
            pl.BlockSpec((tm, d), lambda i, j: (i, 0)),
            mspec(m=3), mspec(m=4),
            pl.BlockSpec((None, None, 1, d), lambda i, j: (layer, 1, 0, 0)),
            pl.BlockSpec((None, None, d, tn), lambda i, j: (layer, j, 0, 0)),
        ],
        out_specs=[
            pl.BlockSpec((tm, tn), lambda i, j: (i, jnp.minimum(j, na - 1))),
            pl.BlockSpec((tm, tn), lambda i, j: (i, jnp.clip(j - na, 0, nkv - 1))),
            pl.BlockSpec((tm, tn), lambda i, j: (i, jnp.clip(j - na - nkv, 0, ng - 1))),
        ],
        out_shape=[
            jax.ShapeDtypeStruct((m_tok, a_cols), BF16),
            jax.ShapeDtypeStruct((m_tok, kv_cols), F32),
            jax.ShapeDtypeStruct((m_tok, g_cols), BF16),
        ],
        scratch_shapes=[pltpu.VMEM((tm, d), BF16)],
        compiler_params=_params(("parallel", "arbitrary")),
        name=f"inproj_l{layer}",
    )(x, mod, mod, norm_g, w_in)


def _conv_tile(i, prev_ref, cur_ref, next_ref, w_ref, b_ref, lg_ref, lb_ref, hs_ref, sh_ref, cv_ref,
               *, tp, dc, prompt_tiles, tiles_per_seq):
    def glu(ref):
        z = ref[...].astype(F32)
        return z[:, :dc] * jax.nn.sigmoid(z[:, dc:])

    pos = lax.rem(i - prompt_tiles, tiles_per_seq)
    is_latent = i >= prompt_tiles
    has_prev = jnp.logical_and(is_latent, pos != 0)
    has_next = jnp.logical_and(is_latent, pos != tiles_per_seq - 1)
    hs_ref[0:HALO, :] = jnp.where(has_prev, glu(prev_ref), 0.0)
    hs_ref[HALO:HALO + tp, :] = glu(cur_ref)
    hs_ref[HALO + tp:2 * HALO + tp, :] = jnp.where(has_next, glu(next_ref), 0.0)

    base = HALO - CONV_WIDTH // 2
    ext = sh_ref.shape[1]
    for s in range(SUBLANES):
        sh_ref[s] = hs_ref[base + s:base + s + ext, :]

    rc = 32
    for c0 in range(0, dc, LANES):
        for r0 in range(0, tp, rc):
            acc = jnp.zeros((rc, LANES), F32)
            for k in range(CONV_WIDTH):
                st = r0 + SUBLANES * (k // SUBLANES)
                acc = acc + sh_ref[k % SUBLANES, st:st + rc, c0:c0 + LANES] * w_ref[k:k + 1, c0:c0 + LANES]
            cv_ref[r0:r0 + rc, c0:c0 + LANES] = acc

    h = cv_ref[...] + b_ref[...]
    mu = jnp.mean(h, axis=-1, keepdims=True)
    hc = h - mu
    var = jnp.mean(hc * hc, axis=-1, keepdims=True)
    y = hc * lax.rsqrt(var + EPS) * lg_ref[...] + lb_ref[...]
    return _silu(y)


def _dot_nt(a, b):
    return lax.dot_general(a, b, (((1,), (1,)), ((), ())), preferred_element_type=F32)


def _ctx_attn_kernel(q_ref, k_ref, v_ref, o_ref):
    scale = HEAD_DIM ** -0.5
    outs = []
    for h in range(N_HEADS):
        sl = slice(h * HEAD_DIM, (h + 1) * HEAD_DIM)
        q = q_ref[:, sl]
        k = k_ref[:, sl].astype(BF16)
        v = v_ref[:, sl].astype(BF16)
        s = _dot_nt(q, k) * scale
        e = jnp.exp(s - jnp.max(s, axis=-1, keepdims=True))
        p = e / jnp.sum(e, axis=-1, keepdims=True)
        outs.append(jnp.dot(p.astype(BF16), v, preferred_element_type=F32))
    o_ref[...] = jnp.concatenate(outs, axis=-1).astype(o_ref.dtype)


def _ctx_attention(a, kv, batch, seq, q_col0):
    d_na = N_HEADS * HEAD_DIM
    qb = q_col0 // d_na
    return pl.pallas_call(
        _ctx_attn_kernel,
        grid=(batch,),
        in_specs=[
            pl.BlockSpec((seq, d_na), lambda b: (b, qb)),
            pl.BlockSpec((seq, d_na), lambda b: (b, 0)),
            pl.BlockSpec((seq, d_na), lambda b: (b, 1)),
        ],
        out_specs=pl.BlockSpec((seq, d_na), lambda b: (b, 0)),
        out_shape=jax.ShapeDtypeStruct((batch * seq, d_na), BF16),
        compiler_params=_params(("parallel",)),
        name="ctx_attn",
    )(a, kv, kv)


NA_QROWS = 4
NA_KROWS = NA_QROWS + WIN_ROWS


def _na_kernel(q_ref, k_ref, v_ref, ck_ref, cv_ref, bias_ref, o_ref, *, mb, rows):
    n_blk = rows // NA_QROWS
    qn = NA_QROWS * GRID_W
    kn = NA_KROWS * GRID_W
    first = lax.broadcasted_iota(jnp.int32, (qn, 2 * HEAD_DIM), 1) < HEAD_DIM
    ck = ck_ref[...]
    cv = cv_ref[...]
    for mi in range(mb):
        m = pl.program_id(2) * mb + mi
        kr0 = jnp.clip(NA_QROWS * m - WIN_ROWS // 2, 0, rows - NA_KROWS)
        k0 = pl.multiple_of(kr0 * GRID_W, NA_QROWS * GRID_W)
        ku = k_ref[pl.ds(k0, kn), :].astype(BF16)
        vu = v_ref[pl.ds(k0, kn), :].astype(BF16)
        variant = jnp.where(m == 0, 0, jnp.where(m == n_blk - 1, 2, 1))
        q = q_ref[mi * qn:(mi + 1) * qn, :].astype(F32) * (HEAD_DIM ** -0.5)
        out = None
        for hh in range(2):
            own = first if hh == 0 else jnp.logical_not(first)
            qh = jnp.where(own, q, 0.0).astype(BF16)
            s_loc = _dot_nt(qh, ku) + bias_ref[variant, hh]
            s_ctx = _dot_nt(qh, ck)
            mx = jnp.maximum(jnp.max(s_loc, axis=-1, keepdims=True), jnp.max(s_ctx, axis=-1, keepdims=True))
            e_loc = jnp.exp(s_loc - mx)
            e_ctx = jnp.exp(s_ctx - mx)
            denom = jnp.sum(e_loc, axis=-1, keepdims=True) + jnp.sum(e_ctx, axis=-1, keepdims=True)
            o = (jnp.dot(e_loc.astype(BF16), vu, preferred_element_type=F32)
                 + jnp.dot(e_ctx.astype(BF16), cv, preferred_element_type=F32)) / denom
            out = o if hh == 0 else jnp.where(first, out, o)
        o_ref[mi * qn:(mi + 1) * qn, :] = out.astype(o_ref.dtype)


def _na_bias_table(rel_bias, rows):
    n_blk = rows // NA_QROWS
    assert rows % NA_QROWS == 0 and n_blk >= 3 and rows >= NA_KROWS + WIN_ROWS // 2
    depth, n_heads = rel_bias.shape[:2]
    n_drow = 2 * WIN_ROWS - 1
    w = jnp.arange(GRID_W)[:, None]
    j = jnp.arange(GRID_W)[None, :]
    cs = jnp.clip(w - WIN_COLS // 2, 0, GRID_W - WIN_COLS)
    vcol = (j >= cs) & (j < cs + WIN_COLS)
    dcol = j - w + (WIN_COLS - 1)
    pick = ((dcol[None] == jnp.arange(2 * WIN_COLS - 1)[:, None, None]) & vcol[None]).astype(F32)
    cols = jnp.einsum("lhdc,cwj->lhwdj", rel_bias, pick, precision=lax.Precision.HIGHEST)
    cols = jnp.where(vcol[:, None, :], cols, NEG_INF).reshape(depth, n_heads, GRID_W, n_drow * GRID_W)

    def masked(n):
        return jnp.full((depth, n_heads, GRID_W, n * GRID_W), NEG_INF, F32)

    tabs = []
    for m in (0, 1, n_blk - 1):
        kr0 = min(max(NA_QROWS * m - WIN_ROWS // 2, 0), rows - NA_KROWS)
        blocks = []
        for rq in range(NA_QROWS):
            r = NA_QROWS * m + rq
            rs = min(max(r - WIN_ROWS // 2, 0), rows - WIN_ROWS)
            u0 = rs - kr0
            d0 = rs - r + (WIN_ROWS - 1)
            parts = [masked(u0), cols[..., d0 * GRID_W:(d0 + WIN_ROWS) * GRID_W],
                     masked(NA_KROWS - WIN_ROWS - u0)]
            blocks.append(jnp.concatenate([p for p in parts if p.shape[-1]], axis=-1))
        tabs.append(jnp.concatenate(blocks, axis=2))
    return jnp.stack(tabs, axis=1)


def _na_attention(a, kv, ck, cv, bias, layer, n_prompt, dec_batch, dec_seq, q_col0, mb=8):
    d_na = N_HEADS * HEAD_DIM
    rows = dec_seq // GRID_W
    hpw = 2 * HEAD_DIM
    qrows = mb * NA_QROWS * GRID_W
    steps = dec_seq // qrows
    past = ck.shape[2]
    assert n_prompt % qrows == 0 and n_prompt % dec_seq == 0 and dec_seq % qrows == 0
    kern = functools.partial(_na_kernel, mb=mb, rows=rows)
    q_blk0 = n_prompt // qrows
    kv_blk0 = n_prompt // dec_seq
    return pl.pallas_call(
        kern,
        grid=(N_HEADS // 2, dec_batch, steps),
        in_specs=[
            pl.BlockSpec((qrows, hpw), lambda hp, b, r: (q_blk0 + b * steps + r, q_col0 // hpw + hp)),
            pl.BlockSpec((dec_seq, hpw), lambda hp, b, r: (kv_blk0 + b, hp)),
            pl.BlockSpec((dec_seq, hpw), lambda hp, b, r: (kv_blk0 + b, d_na // hpw + hp)),
            pl.BlockSpec((None, None, past, hpw), lambda hp, b, r: (b, layer, 0, hp)),
            pl.BlockSpec((None, None, past, hpw), lambda hp, b, r: (b, layer, 0, hp)),
            pl.BlockSpec((None, 3, 2, NA_QROWS * GRID_W, NA_KROWS * GRID_W), lambda hp, b, r: (layer, 0, hp, 0, 0)),
        ],
        out_specs=pl.BlockSpec((qrows, hpw), lambda hp, b, r: (b * steps + r, hp)),
        out_shape=jax.ShapeDtypeStruct((dec_batch * dec_seq, d_na), BF16),
        compiler_params=_params(("parallel", "parallel", "arbitrary")),
        name=f"na_l{layer}",
    )(a, kv, kv, ck, cv, bias)


def _merge_conv_kernel(x_ref, gate_ref, nop_ref, nos_ref, gc_ref, gn_ref, wc_ref, wn_ref, wo_ref,
                       prev_ref, cur_ref, next_ref, cw_ref, cb_ref, lg_ref, lb_ref,
                       o_ref, hs_ref, sh_ref, cv_ref, ch_ref, *, n_tiles, prompt_tiles, conv_kw):
    s = pl.program_id(0)

    @pl.when(s == 0)
    def _():
        ch_ref[...] = jnp.zeros_like(ch_ref)

    ch = ch_ref[(s + 1) % 2]
    no = jnp.where(jnp.maximum(s - 1, 0) < prompt_tiles, nop_ref[...], nos_ref[...])
    yc = jnp.dot(ch, wc_ref[...], preferred_element_type=F32)
    yn = jnp.dot(no, wn_ref[...], preferred_element_type=F32)
    m = (jax.nn.sigmoid(gc_ref[...].astype(F32)) * yc + jax.nn.sigmoid(gn_ref[...].astype(F32)) * yn)
    o_ref[...] = x_ref[...] + gate_ref[...] * jnp.dot(m.astype(BF16), wo_ref[...], preferred_element_type=F32)

    ch_new = _conv_tile(jnp.minimum(s, n_tiles - 1), prev_ref, cur_ref, next_ref, cw_ref, cb_ref, lg_ref, lb_ref,
                        hs_ref, sh_ref, cv_ref, **conv_kw)
    ch_ref[s % 2] = ch_new.astype(ch_ref.dtype)


def _merge_conv(x, mod, a, na_p, na_s, gates, conv_w, conv_b, ln_g, ln_b, w_conv_out, w_na_out, w_out,
                layer, n_prompt, seq, dec_seq, tm=256):
    m_tok, d = x.shape
    dc = conv_w.shape[-1]
    dn = na_p.shape[1]
    assert seq == tm and dec_seq % tm == 0 and n_prompt % tm == 0
    n_tiles = m_tok // tm
    pt = n_prompt // tm
    hb = tm // HALO
    n_hblk = m_tok // HALO

    def mt(s):
        return jnp.maximum(s - 1, 0)

    def ct(s):
        return jnp.minimum(s, n_tiles - 1)

    def gate_idx(s):
        row0 = mt(s) * tm
        return (layer, jnp.where(row0 < n_prompt, 0, 1 + (row0 - n_prompt) // dec_seq), 5, 0, 0)

    vec = pl.BlockSpec((None, 1, dc), lambda s: (layer, 0, 0))
    conv_kw = dict(tp=tm, dc=dc, prompt_tiles=pt, tiles_per_seq=dec_seq // tm)
    return pl.pallas_call(
        functools.partial(_merge_conv_kernel, n_tiles=n_tiles, prompt_tiles=pt, conv_kw=conv_kw),
        grid=(n_tiles + 1,),
        in_specs=[
            pl.BlockSpec((tm, d), lambda s: (mt(s), 0)),
            pl.BlockSpec((None, None, None, 1, d), gate_idx),
            pl.BlockSpec((tm, dn), lambda s: (jnp.minimum(mt(s), pt - 1), 0)),
            pl.BlockSpec((tm, dn), lambda s: (jnp.maximum(mt(s) - pt, 0), 0)),
            pl.BlockSpec((tm, d), lambda s: (mt(s), 0)),
            pl.BlockSpec((tm, d), lambda s: (mt(s), 1)),
            pl.BlockSpec((None, dc, d), lambda s: (layer, 0, 0)),
            pl.BlockSpec((None, dn, d), lambda s: (layer, 0, 0)),
            pl.BlockSpec((None, d, d), lambda s: (layer, 0, 0)),
            pl.BlockSpec((HALO, 2 * dc), lambda s: (jnp.maximum(ct(s) * hb - 1, 0), 0)),
            pl.BlockSpec((tm, 2 * dc), lambda s: (ct(s), 0)),
            pl.BlockSpec((HALO, 2 * dc), lambda s: (jnp.minimum((ct(s) + 1) * hb, n_hblk - 1), 0)),
            pl.BlockSpec((None, CONV_WIDTH, dc), lambda s: (layer, 0, 0)),
            vec, vec, vec,
        ],
        out_specs=pl.BlockSpec((tm, d), lambda s: (mt(s), 0)),
        out_shape=jax.ShapeDtypeStruct((m_tok, d), F32),
        scratch_shapes=[
            pltpu.VMEM((tm + 2 * HALO, dc), F32),
            pltpu.VMEM((SUBLANES, tm + SUBLANES * ((CONV_WIDTH - 1) // SUBLANES), dc), F32),
            pltpu.VMEM((tm, dc), F32),
            pltpu.VMEM((2, tm, dc), BF16),
        ],
        compiler_params=_params(("arbitrary",)),
        name=f"merge_conv_l{layer}",
    )(x, mod, na_p, na_s, gates, gates, w_conv_out, w_na_out, w_out, a, a, a, conv_w,
      conv_b.reshape(-1, 1, dc), ln_g.reshape(-1, 1, dc), ln_b.reshape(-1, 1, dc))


def kernel(x_prompt, x_sample, cache_k, cache_v, c, c_ctx, w_ada, b_ada, norm_g, ffn_w_gate, ffn_w_up,
           ffn_w_down, w_in, conv_w, conv_b, conv_ln_g, conv_ln_b, w_conv_out, rel_bias, w_na_out, w_out,
           final_g):
    batch, seq, d = x_prompt.shape
    dec_batch, dec_seq, _ = x_sample.shape
    depth = w_ada.shape[0]
    d_ff = ffn_w_gate.shape[-1]
    d_na = N_HEADS * HEAD_DIM
    dc = conv_w.shape[-1]
    past = cache_k.shape[2]
    n_prompt = batch * seq
    n_latent = dec_batch * dec_seq

    cvec = jnp.concatenate([c_ctx[None, :], c, jnp.zeros((MOD_ROWS - 1 - dec_batch, d), F32)], axis=0)
    mod = _ada_table(cvec, w_ada, b_ada).reshape(depth, MOD_ROWS, N_MOD, 1, d)

    wg = _bf16_col_chunks(ffn_w_gate.reshape(depth * 2, d, d_ff), FFN_CHUNK)
    wu = _bf16_col_chunks(ffn_w_up.reshape(depth * 2, d, d_ff), FFN_CHUNK)
    wd = _bf16_row_pad(ffn_w_down.reshape(depth * 2, d_ff, d), FFN_CHUNK)
    w_in_b = _bf16_col_chunks(w_in, INPROJ_CHUNK)
    w_co_b = w_conv_out.astype(BF16)
    w_no_b = w_na_out.astype(BF16)
    w_o_b = w_out.astype(BF16)
    ck = cache_k.reshape(dec_batch, depth, past, d_na).astype(BF16)
    cv = cache_v.reshape(dec_batch, depth, past, d_na).astype(BF16)
    norm_g4 = norm_g.reshape(depth, 3, 1, d)

    a_cols = 2 * dc + d_na
    na_bias = _na_bias_table(rel_bias, dec_seq // GRID_W)

    xs = (x_prompt.reshape(n_prompt, d), x_sample.reshape(n_latent, d))
    keys, vals = [], []
    for l in range(depth):
        x = _ffn(xs, mod, norm_g4, wg, wu, wd, l, 0, 0, n_prompt, dec_seq)

        a, kv, gates = _inproj(x, mod, norm_g4, w_in_b, l, a_cols, 2 * d_na, 2 * d, n_prompt, dec_seq)
        keys.append(kv[:n_prompt, :d_na].reshape(batch, seq, N_HEADS, HEAD_DIM))
        vals.append(kv[:n_prompt, d_na:].reshape(batch, seq, N_HEADS, HEAD_DIM))

        na_p = _ctx_attention(a, kv, batch, seq, 2 * dc)
        na_s = _na_attention(a, kv, ck, cv, na_bias, l, n_prompt, dec_batch, dec_seq, 2 * dc)
        x = _merge_conv(x, mod, a, na_p, na_s, gates, conv_w, conv_b, conv_ln_g, conv_ln_b, w_co_b, w_no_b,
                        w_o_b, l, n_prompt, seq, dec_seq)

        out = _ffn((x,), mod, norm_g4, wg, wu, wd, l, 1, 6, n_prompt, dec_seq,
                   final_g=final_g if l == depth - 1 else None)
        xs = (out,)

    y_prompt, y_sample = out
    return (y_prompt.reshape(batch, seq, d), y_sample.reshape(dec_batch, dec_seq, d),
            jnp.stack(keys, axis=1), jnp.stack(vals, axis=1))
```

```python
import functools

import jax
import jax.numpy as jnp
from jax import lax
from jax.experimental import pallas as pl
from jax.experimental.pallas import tpu as pltpu

F32 = jnp.float32
BF16 = jnp.bfloat16

N_HEADS = 16
HEAD_DIM = 64
GRID_W = 64
WIN_ROWS = 8
WIN_COLS = 16
CONV_WIDTH = 31
N_MOD = 9
FFN_RES = 0.5
EPS = 1e-6
NEG_INF = -1e30

SUBLANES = 8
LANES = 128
FFN_CHUNK = 512
INPROJ_CHUNK = 1024
ADA_CHUNK = 1024
MOD_ROWS = 8
NORM_ROWS = 16
NORM_UNROLL = 8
HALO = 16
VMEM_LIMIT = 56 * 1024 * 1024


def _params(sem, vmem=VMEM_LIMIT):
    return pltpu.CompilerParams(dimension_semantics=sem, vmem_limit_bytes=vmem)


def _silu(x):
    return x * jax.nn.sigmoid(x)


def _prep_kernel(w_ref, o_ref, *, axis, valid, blk):
    w = w_ref[...]
    pos = pl.program_id(1) * blk + lax.broadcasted_iota(jnp.int32, w.shape, axis)
    o_ref[...] = jnp.where(pos < valid, w, 0.0).astype(o_ref.dtype)


def _bf16_col_chunks(w, tn):
    g, k, n = w.shape
    nj = pl.cdiv(n, tn)
    return pl.pallas_call(
        functools.partial(_prep_kernel, axis=1, valid=n, blk=tn),
        grid=(g, nj),
        in_specs=[pl.BlockSpec((None, k, tn), lambda i, j: (i, 0, j))],
        out_specs=pl.BlockSpec((None, None, k, tn), lambda i, j: (i, j, 0, 0)),
        out_shape=jax.ShapeDtypeStruct((g, nj, k, tn), BF16),
        compiler_params=_params(("parallel", "parallel")),
        name="bf16_col_chunks",
    )(w)


def _bf16_row_pad(w, tr):
    g, r, n = w.shape
    nj = pl.cdiv(r, tr)
    return pl.pallas_call(
        functools.partial(_prep_kernel, axis=0, valid=r, blk=tr),
        grid=(g, nj),
        in_specs=[pl.BlockSpec((None, tr, n), lambda i, j: (i, j, 0))],
        out_specs=pl.BlockSpec((None, tr, n), lambda i, j: (i, j, 0)),
        out_shape=jax.ShapeDtypeStruct((g, nj * tr, n), BF16),
        compiler_params=_params(("parallel", "parallel")),
        name="bf16_row_pad",
    )(w)


def _ada_kernel(c_ref, w_ref, b_ref, o_ref):
    s = _silu(c_ref[...]).astype(BF16)
    o_ref[...] = jnp.dot(s, w_ref[...].astype(BF16), preferred_element_type=F32) + b_ref[...]


def _ada_table(cvec, w_ada, b_ada):
    depth, d, n = w_ada.shape
    tn = min(ADA_CHUNK, d)
    assert n % tn == 0
    return pl.pallas_call(
        _ada_kernel,
        grid=(depth, n // tn),
        in_specs=[
            pl.BlockSpec((MOD_ROWS, d), lambda l, j: (0, 0)),
            pl.BlockSpec((None, d, tn), lambda l, j: (l, 0, j)),
            pl.BlockSpec((None, 1, tn), lambda l, j: (l, 0, j)),
        ],
        out_specs=pl.BlockSpec((None, MOD_ROWS, tn), lambda l, j: (l, 0, j)),
        out_shape=jax.ShapeDtypeStruct((depth, MOD_ROWS, n), F32),
        compiler_params=_params(("arbitrary", "arbitrary")),
        name="ada_table",
    )(cvec, w_ada, b_ada.reshape(depth, 1, n))


def _mod_spec(layer, m, tm, d, n_prompt, dec_seq):
    def idx(i, *_):
        row0 = i * tm
        grp = jnp.where(row0 < n_prompt, 0, 1 + (row0 - n_prompt) // dec_seq)
        return (layer, grp, m, 0, 0)
    return pl.BlockSpec((None, None, None, 1, d), idx)


def _store_norm_mod(x_ref, g_ref, shift_ref, scale_ref, hn_ref):
    gain = g_ref[...] * (1.0 + scale_ref[...])
    shift = shift_ref[...]

    def body(i, carry):
        r = pl.multiple_of(i * NORM_ROWS, NORM_ROWS)
        x = x_ref[pl.ds(r, NORM_ROWS), :]
        ms = jnp.mean(x * x, axis=-1, keepdims=True)
        hn_ref[pl.ds(r, NORM_ROWS), :] = (x * lax.rsqrt(ms + EPS) * gain + shift).astype(hn_ref.dtype)
        return carry

    lax.fori_loop(0, x_ref.shape[0] // NORM_ROWS, body, 0, unroll=NORM_UNROLL)


def _ffn_kernel(*refs, n_split, n_x, prompt_tiles, final):
    x_refs, refs = refs[:n_x], refs[n_x:]
    shift_ref, scale_ref, gate_ref, g_ref, wg_ref, wu_ref, wd_ref = refs[:7]
    if final:
        fg_ref, yp_ref, ys_ref, hn_ref, acc_ref = refs[7:]
    else:
        o_ref, hn_ref = refs[7:]
        acc_ref = o_ref
    i = pl.program_id(0)
    j = pl.program_id(1)

    def per_stream(fn_prompt, fn_latent):
        pl.when(i < prompt_tiles)(fn_prompt)
        pl.when(i >= prompt_tiles)(fn_latent)

    def with_x(fn):
        if n_x == 1:
            fn(x_refs[0])
        else:
            per_stream(lambda: fn(x_refs[0]), lambda: fn(x_refs[1]))

    @pl.when(j == 0)
    def _():
        with_x(lambda x_ref: _store_norm_mod(x_ref, g_ref, shift_ref, scale_ref, hn_ref))
        acc_ref[...] = jnp.zeros_like(acc_ref)

    hn = hn_ref[...]
    cw = wg_ref.shape[-1] // n_split
    part = None
    for c in range(n_split):
        sl = slice(c * cw, (c + 1) * cw)
        a = jnp.dot(hn, wg_ref[:, sl], preferred_element_type=F32)
        u = jnp.dot(hn, wu_ref[:, sl], preferred_element_type=F32)
        h = (_silu(a) * u).astype(BF16)
        p = jnp.dot(h, wd_ref[sl, :], preferred_element_type=F32)
        part = p if part is None else part + p
    acc_ref[...] += part

    @pl.when(j == pl.num_programs(1) - 1)
    def _():
        half_gate = FFN_RES * gate_ref[...]

        def finish(x_ref):
            if not final:
                o_ref[...] = x_ref[...] + half_gate * acc_ref[...]
                return

            def normed_to(y_ref):
                def body(t, carry):
                    rs = pl.ds(pl.multiple_of(t * NORM_ROWS, NORM_ROWS), NORM_ROWS)
                    y = x_ref[rs, :] + half_gate * acc_ref[rs, :]
                    ms = jnp.mean(y * y, axis=-1, keepdims=True)
                    y_ref[rs, :] = (y * lax.rsqrt(ms + EPS)) * fg_ref[...]
                    return carry

                lax.fori_loop(0, x_ref.shape[0] // NORM_ROWS, body, 0, unroll=NORM_UNROLL)

            per_stream(lambda: normed_to(yp_ref), lambda: normed_to(ys_ref))

        with_x(finish)


def _ffn(xs, mod, norm_g, wg, wu, wd, layer, sub, mod0, n_prompt, dec_seq, final_g=None, tm=512, n_split=2):
    d = xs[0].shape[1]
    m_tok = sum(x.shape[0] for x in xs)
    n_chunks, tf = wg.shape[1], wg.shape[-1]
    ls = 2 * layer + sub
    pt = n_prompt // tm
    mspec = functools.partial(_mod_spec, layer, tm=tm, d=d, n_prompt=n_prompt, dec_seq=dec_seq)
    split_specs = [pl.BlockSpec((tm, d), lambda i, j: (jnp.minimum(i, pt - 1), 0)),
                   pl.BlockSpec((tm, d), lambda i, j: (jnp.maximum(i - pt, 0), 0))]
    whole_spec = pl.BlockSpec((tm, d), lambda i, j: (i, 0))
    in_specs = (split_specs if len(xs) == 2 else [whole_spec]) + [
        mspec(m=mod0), mspec(m=mod0 + 1), mspec(m=mod0 + 2),
        pl.BlockSpec((None, None, 1, d), lambda i, j: (layer, 2 * sub, 0, 0)),
        pl.BlockSpec((None, None, d, tf), lambda i, j: (ls, j, 0, 0)),
        pl.BlockSpec((None, None, d, tf), lambda i, j: (ls, j, 0, 0)),
        pl.BlockSpec((None, tf, d), lambda i, j: (ls, j, 0)),
    ]
    args = list(xs) + [mod, mod, mod, norm_g, wg, wu, wd]
    scratch = [pltpu.VMEM((tm, d), BF16)]
    if final_g is None:
        out_specs = whole_spec
        out_shape = jax.ShapeDtypeStruct((m_tok, d), F32)
    else:
        in_specs.append(pl.BlockSpec((1, d), lambda i, j: (0, 0)))
        args.append(final_g.reshape(1, d))
        out_specs = split_specs
        out_shape = [jax.ShapeDtypeStruct((n_prompt, d), F32), jax.ShapeDtypeStruct((m_tok - n_prompt, d), F32)]
        scratch.append(pltpu.VMEM((tm, d), F32))
    return pl.pallas_call(
        functools.partial(_ffn_kernel, n_split=n_split, n_x=len(xs), prompt_tiles=pt, final=final_g is not None),
        grid=(m_tok // tm, n_chunks),
        in_specs=in_specs,
        out_specs=out_specs,
        out_shape=out_shape,
        scratch_shapes=scratch,
        compiler_params=_params(("arbitrary" if final_g is not None else "parallel", "arbitrary")),
        name=f"ffn_l{layer}_s{sub}",
    )(*args)


def _inproj_kernel(x_ref, shift_ref, scale_ref, g_ref, w_ref, a_ref, kv_ref, gt_ref, hn_ref,
                   *, na, nkv, n_split):
    j = pl.program_id(1)

    @pl.when(j == 0)
    def _():
        _store_norm_mod(x_ref, g_ref, shift_ref, scale_ref, hn_ref)

    cw = w_ref.shape[-1] // n_split

    def project(o_ref):
        hn = hn_ref[...]
        for c in range(n_split):
            cs = slice(c * cw, (c + 1) * cw)
            o_ref[:, cs] = jnp.dot(hn, w_ref[:, cs], preferred_element_type=F32).astype(o_ref.dtype)

    @pl.when(j < na)
    def _():
        project(a_ref)

    @pl.when(jnp.logical_and(j >= na, j < na + nkv))
    def _():
        project(kv_ref)

    @pl.when(j >= na + nkv)
    def _():
        project(gt_ref)


def _inproj(x, mod, norm_g, w_in, layer, a_cols, kv_cols, g_cols, n_prompt, dec_seq, tm=1024, n_split=2):
    m_tok, d = x.shape
    n_chunks, tn = w_in.shape[1], w_in.shape[-1]
    na, nkv, ng = a_cols // tn, kv_cols // tn, g_cols // tn
    assert na * tn == a_cols and nkv * tn == kv_cols and ng * tn == g_cols and na + nkv + ng == n_chunks
    mspec = functools.partial(_mod_spec, layer, tm=tm, d=d, n_prompt=n_prompt, dec_seq=dec_seq)
    return pl.pallas_call(
        functools.partial(_inproj_kernel, na=na, nkv=nkv, n_split=n_split),
        grid=(m_tok // tm, n_chunks),
        in_specs=[
            pl.BlockSpec((tm, d), lambda i, j: (i, 0)),
            mspec(m=3), mspec(m=4),
            pl.BlockSpec((None, None, 1, d), lambda i, j: (layer, 1, 0, 0)),
            pl.BlockSpec((None, None, d, tn), lambda i, j: (layer, j, 0, 0)),
        ],
        out_specs=[
            pl.BlockSpec((tm, tn), lambda i, j: (i, jnp.minimum(j, na - 1))),
            pl.BlockSpec((tm, tn), lambda i, j: (i, jnp.clip(j - na, 0, nkv - 1))),
            pl.BlockSpec((tm, tn), lambda i, j: (i, jnp.clip(j - na - nkv, 0, ng - 1))),
        ],
        out_shape=[
            jax.ShapeDtypeStruct((m_tok, a_cols), BF16),
            jax.ShapeDtypeStruct((m_tok, kv_cols), F32),
            jax.ShapeDtypeStruct((m_tok, g_cols), BF16),
        ],
        scratch_shapes=[pltpu.VMEM((tm, d), BF16)],
        compiler_params=_params(("parallel", "arbitrary")),
        name=f"inproj_l{layer}",
    )(x, mod, mod, norm_g, w_in)


def _conv_tile(i, prev_ref, cur_ref, next_ref, w_ref, b_ref, lg_ref, lb_ref, hs_ref, sh_ref, cv_ref,
               *, tp, dc, prompt_tiles, tiles_per_seq):
    def glu(ref):
        z = ref[...].astype(F32)
        return z[:, :dc] * jax.nn.sigmoid(z[:, dc:])

    pos = lax.rem(i - prompt_tiles, tiles_per_seq)
    is_latent = i >= prompt_tiles
    has_prev = jnp.logical_and(is_latent, pos != 0)
    has_next = jnp.logical_and(is_latent, pos != tiles_per_seq - 1)
    hs_ref[0:HALO, :] = jnp.where(has_prev, glu(prev_ref), 0.0)
    hs_ref[HALO:HALO + tp, :] = glu(cur_ref)
    hs_ref[HALO + tp:2 * HALO + tp, :] = jnp.where(has_next, glu(next_ref), 0.0)

    base = HALO - CONV_WIDTH // 2
    ext = sh_ref.shape[1]
    for s in range(SUBLANES):
        sh_ref[s] = hs_ref[base + s:base + s + ext, :]

    rc = 32
    for c0 in range(0, dc, LANES):
        for r0 in range(0, tp, rc):
            acc = jnp.zeros((rc, LANES), F32)
            for k in range(CONV_WIDTH):
                st = r0 + SUBLANES * (k // SUBLANES)
                acc = acc + sh_ref[k % SUBLANES, st:st + rc, c0:c0 + LANES] * w_ref[k:k + 1, c0:c0 + LANES]
            cv_ref[r0:r0 + rc, c0:c0 + LANES] = acc

    h = cv_ref[...] + b_ref[...]
    mu = jnp.mean(h, axis=-1, keepdims=True)
    hc = h - mu
    var = jnp.mean(hc * hc, axis=-1, keepdims=True)
    y = hc * lax.rsqrt(var + EPS) * lg_ref[...] + lb_ref[...]
    return _silu(y)


def _dot_nt(a, b):
    return lax.dot_general(a, b, (((1,), (1,)), ((), ())), preferred_element_type=F32)


def _ctx_attn_kernel(q_ref, k_ref, v_ref, o_ref):
    pw = 2 * HEAD_DIM
    first = lax.broadcasted_iota(jnp.int32, (q_ref.shape[0], pw), 1) < HEAD_DIM
    for hp in range(N_HEADS // 2):
        sl = slice(hp * pw, (hp + 1) * pw)
        q = q_ref[:, sl].astype(F32) * (HEAD_DIM ** -0.5)
        k = k_ref[:, sl].astype(BF16)
        v = v_ref[:, sl].astype(BF16)
        out = None
        for hh in range(2):
            own = first if hh == 0 else jnp.logical_not(first)
            s = _dot_nt(jnp.where(own, q, 0.0).astype(BF16), k)
            e = jnp.exp(s - jnp.max(s, axis=-1, keepdims=True))
            o = jnp.dot(e.astype(BF16), v, preferred_element_type=F32) / jnp.sum(e, axis=-1, keepdims=True)
            out = o if hh == 0 else jnp.where(first, out, o)
        o_ref[:, sl] = out.astype(o_ref.dtype)


def _ctx_attention(a, kv, batch, seq, q_col0):
    d_na = N_HEADS * HEAD_DIM
    qb = q_col0 // d_na
    return pl.pallas_call(
        _ctx_attn_kernel,
        grid=(batch,),
        in_specs=[
            pl.BlockSpec((seq, d_na), lambda b: (b, qb)),
            pl.BlockSpec((seq, d_na), lambda b: (b, 0)),
            pl.BlockSpec((seq, d_na), lambda b: (b, 1)),
        ],
        out_specs=pl.BlockSpec((seq, d_na), lambda b: (b, 0)),
        out_shape=jax.ShapeDtypeStruct((batch * seq, d_na), BF16),
        compiler_params=_params(("parallel",)),
        name="ctx_attn",
    )(a, kv, kv)


NA_QROWS = 4
NA_KROWS = NA_QROWS + WIN_ROWS


def _na_kernel(q_ref, k_ref, v_ref, ck_ref, cv_ref, bias_ref, o_ref, *, mb, rows):
    n_blk = rows // NA_QROWS
    qn = NA_QROWS * GRID_W
    kn = NA_KROWS * GRID_W
    first = lax.broadcasted_iota(jnp.int32, (qn, 2 * HEAD_DIM), 1) < HEAD_DIM
    ck = ck_ref[...]
    cv = cv_ref[...]
    for mi in range(mb):
        m = pl.program_id(2) * mb + mi
        kr0 = jnp.clip(NA_QROWS * m - WIN_ROWS // 2, 0, rows - NA_KROWS)
        k0 = pl.multiple_of(kr0 * GRID_W, NA_QROWS * GRID_W)
        ku = k_ref[pl.ds(k0, kn), :].astype(BF16)
        vu = v_ref[pl.ds(k0, kn), :].astype(BF16)
        variant = jnp.where(m == 0, 0, jnp.where(m == n_blk - 1, 2, 1))
        q = q_ref[mi * qn:(mi + 1) * qn, :].astype(F32) * (HEAD_DIM ** -0.5)
        out = None
        for hh in range(2):
            own = first if hh == 0 else jnp.logical_not(first)
            qh = jnp.where(own, q, 0.0).astype(BF16)
            s_loc = _dot_nt(qh, ku) + bias_ref[variant, hh]
            s_ctx = _dot_nt(qh, ck)
            mx = jnp.maximum(jnp.max(s_loc, axis=-1, keepdims=True), jnp.max(s_ctx, axis=-1, keepdims=True))
            e_loc = jnp.exp(s_loc - mx)
            e_ctx = jnp.exp(s_ctx - mx)
            denom = jnp.sum(e_loc, axis=-1, keepdims=True) + jnp.sum(e_ctx, axis=-1, keepdims=True)
            o = (jnp.dot(e_loc.astype(BF16), vu, preferred_element_type=F32)
                 + jnp.dot(e_ctx.astype(BF16), cv, preferred_element_type=F32)) / denom
            out = o if hh == 0 else jnp.where(first, out, o)
        o_ref[mi * qn:(mi + 1) * qn, :] = out.astype(o_ref.dtype)


def _na_bias_table(rel_bias, rows):
    n_blk = rows // NA_QROWS
    assert rows % NA_QROWS == 0 and n_blk >= 3 and rows >= NA_KROWS + WIN_ROWS // 2
    depth, n_heads = rel_bias.shape[:2]
    n_drow = 2 * WIN_ROWS - 1
    w = jnp.arange(GRID_W)[:, None]
    j = jnp.arange(GRID_W)[None, :]
    cs = jnp.clip(w - WIN_COLS // 2, 0, GRID_W - WIN_COLS)
    vcol = (j >= cs) & (j < cs + WIN_COLS)
    dcol = j - w + (WIN_COLS - 1)
    pick = ((dcol[None] == jnp.arange(2 * WIN_COLS - 1)[:, None, None]) & vcol[None]).astype(F32)
    cols = jnp.einsum("lhdc,cwj->lhwdj", rel_bias, pick, precision=lax.Precision.HIGHEST)
    cols = jnp.where(vcol[:, None, :], cols, NEG_INF).reshape(depth, n_heads, GRID_W, n_drow * GRID_W)

    def masked(n):
        return jnp.full((depth, n_heads, GRID_W, n * GRID_W), NEG_INF, F32)

    tabs = []
    for m in (0, 1, n_blk - 1):
        kr0 = min(max(NA_QROWS * m - WIN_ROWS // 2, 0), rows - NA_KROWS)
        blocks = []
        for rq in range(NA_QROWS):
            r = NA_QROWS * m + rq
            rs = min(max(r - WIN_ROWS // 2, 0), rows - WIN_ROWS)
            u0 = rs - kr0
            d0 = rs - r + (WIN_ROWS - 1)
            parts = [masked(u0), cols[..., d0 * GRID_W:(d0 + WIN_ROWS) * GRID_W],
                     masked(NA_KROWS - WIN_ROWS - u0)]
            blocks.append(jnp.concatenate([p for p in parts if p.shape[-1]], axis=-1))
        tabs.append(jnp.concatenate(blocks, axis=2))
    return jnp.stack(tabs, axis=1)


def _na_attention(a, kv, ck, cv, bias, layer, n_prompt, dec_batch, dec_seq, q_col0, mb=8):
    d_na = N_HEADS * HEAD_DIM
    rows = dec_seq // GRID_W
    hpw = 2 * HEAD_DIM
    qrows = mb * NA_QROWS * GRID_W
    steps = dec_seq // qrows
    past = ck.shape[2]
    assert n_prompt % qrows == 0 and n_prompt % dec_seq == 0 and dec_seq % qrows == 0
    kern = functools.partial(_na_kernel, mb=mb, rows=rows)
    q_blk0 = n_prompt // qrows
    kv_blk0 = n_prompt // dec_seq
    return pl.pallas_call(
        kern,
        grid=(N_HEADS // 2, dec_batch, steps),
        in_specs=[
            pl.BlockSpec((qrows, hpw), lambda hp, b, r: (q_blk0 + b * steps + r, q_col0 // hpw + hp)),
            pl.BlockSpec((dec_seq, hpw), lambda hp, b, r: (kv_blk0 + b, hp)),
            pl.BlockSpec((dec_seq, hpw), lambda hp, b, r: (kv_blk0 + b, d_na // hpw + hp)),
            pl.BlockSpec((None, None, past, hpw), lambda hp, b, r: (b, layer, 0, hp)),
            pl.BlockSpec((None, None, past, hpw), lambda hp, b, r: (b, layer, 0, hp)),
            pl.BlockSpec((None, 3, 2, NA_QROWS * GRID_W, NA_KROWS * GRID_W), lambda hp, b, r: (layer, 0, hp, 0, 0)),
        ],
        out_specs=pl.BlockSpec((qrows, hpw), lambda hp, b, r: (b * steps + r, hp)),
        out_shape=jax.ShapeDtypeStruct((dec_batch * dec_seq, d_na), BF16),
        compiler_params=_params(("parallel", "parallel", "arbitrary")),
        name=f"na_l{layer}",
    )(a, kv, kv, ck, cv, bias)


def _merge_conv_kernel(x_ref, gate_ref, nop_ref, nos_ref, gc_ref, gn_ref, wc_ref, wn_ref, wo_ref,
                       prev_ref, cur_ref, next_ref, cw_ref, cb_ref, lg_ref, lb_ref,
                       o_ref, hs_ref, sh_ref, cv_ref, ch_ref, *, n_tiles, prompt_tiles, conv_kw):
    s = pl.program_id(0)

    @pl.when(s == 0)
    def _():
        ch_ref[...] = jnp.zeros_like(ch_ref)

    ch = ch_ref[(s + 1) % 2]
    no = jnp.where(jnp.maximum(s - 1, 0) < prompt_tiles, nop_ref[...], nos_ref[...])
    yc = jnp.dot(ch, wc_ref[...], preferred_element_type=F32)
    yn = jnp.dot(no, wn_ref[...], preferred_element_type=F32)
    m = (jax.nn.sigmoid(gc_ref[...].astype(F32)) * yc + jax.nn.sigmoid(gn_ref[...].astype(F32)) * yn)
    o_ref[...] = x_ref[...] + gate_ref[...] * jnp.dot(m.astype(BF16), wo_ref[...], preferred_element_type=F32)

    ch_new = _conv_tile(jnp.minimum(s, n_tiles - 1), prev_ref, cur_ref, next_ref, cw_ref, cb_ref, lg_ref, lb_ref,
                        hs_ref, sh_ref, cv_ref, **conv_kw)
    ch_ref[s % 2] = ch_new.astype(ch_ref.dtype)


def _merge_conv(x, mod, a, na_p, na_s, gates, conv_w, conv_b, ln_g, ln_b, w_conv_out, w_na_out, w_out,
                layer, n_prompt, seq, dec_seq, tm=256):
    m_tok, d = x.shape
    dc = conv_w.shape[-1]
    dn = na_p.shape[1]
    assert seq == tm and dec_seq % tm == 0 and n_prompt % tm == 0
    n_tiles = m_tok // tm
    pt = n_prompt // tm
    hb = tm // HALO
    n_hblk = m_tok // HALO

    def mt(s):
        return jnp.maximum(s - 1, 0)

    def ct(s):
        return jnp.minimum(s, n_tiles - 1)

    def gate_idx(s):
        row0 = mt(s) * tm
        return (layer, jnp.where(row0 < n_prompt, 0, 1 + (row0 - n_prompt) // dec_seq), 5, 0, 0)

    vec = pl.BlockSpec((None, 1, dc), lambda s: (layer, 0, 0))
    conv_kw = dict(tp=tm, dc=dc, prompt_tiles=pt, tiles_per_seq=dec_seq // tm)
    return pl.pallas_call(
        functools.partial(_merge_conv_kernel, n_tiles=n_tiles, prompt_tiles=pt, conv_kw=conv_kw),
        grid=(n_tiles + 1,),
        in_specs=[
            pl.BlockSpec((tm, d), lambda s: (mt(s), 0)),
            pl.BlockSpec((None, None, None, 1, d), gate_idx),
            pl.BlockSpec((tm, dn), lambda s: (jnp.minimum(mt(s), pt - 1), 0)),
            pl.BlockSpec((tm, dn), lambda s: (jnp.maximum(mt(s) - pt, 0), 0)),
            pl.BlockSpec((tm, d), lambda s: (mt(s), 0)),
            pl.BlockSpec((tm, d), lambda s: (mt(s), 1)),
            pl.BlockSpec((None, dc, d), lambda s: (layer, 0, 0)),
            pl.BlockSpec((None, dn, d), lambda s: (layer, 0, 0)),
            pl.BlockSpec((None, d, d), lambda s: (layer, 0, 0)),
            pl.BlockSpec((HALO, 2 * dc), lambda s: (jnp.maximum(ct(s) * hb - 1, 0), 0)),
            pl.BlockSpec((tm, 2 * dc), lambda s: (ct(s), 0)),
            pl.BlockSpec((HALO, 2 * dc), lambda s: (jnp.minimum((ct(s) + 1) * hb, n_hblk - 1), 0)),
            pl.BlockSpec((None, CONV_WIDTH, dc), lambda s: (layer, 0, 0)),
            vec, vec, vec,
        ],
        out_specs=pl.BlockSpec((tm, d), lambda s: (mt(s), 0)),
        out_shape=jax.ShapeDtypeStruct((m_tok, d), F32),
        scratch_shapes=[
            pltpu.VMEM((tm + 2 * HALO, dc), F32),
            pltpu.VMEM((SUBLANES, tm + SUBLANES * ((CONV_WIDTH - 1) // SUBLANES), dc), F32),
            pltpu.VMEM((tm, dc), F32),
            pltpu.VMEM((2, tm, dc), BF16),
        ],
        compiler_params=_params(("arbitrary",)),
        name=f"merge_conv_l{layer}",
    )(x, mod, na_p, na_s, gates, gates, w_conv_out, w_na_out, w_out, a, a, a, conv_w,
      conv_b.reshape(-1, 1, dc), ln_g.reshape(-1, 1, dc), ln_b.reshape(-1, 1, dc))


def kernel(x_prompt, x_sample, cache_k, cache_v, c, c_ctx, w_ada, b_ada, norm_g, ffn_w_gate, ffn_w_up,
           ffn_w_down, w_in, conv_w, conv_b, conv_ln_g, conv_ln_b, w_conv_out, rel_bias, w_na_out, w_out,
           final_g):
    batch, seq, d = x_prompt.shape
    dec_batch, dec_seq, _ = x_sample.shape
    depth = w_ada.shape[0]
    d_ff = ffn_w_gate.shape[-1]
    d_na = N_HEADS * HEAD_DIM
    dc = conv_w.shape[-1]
    past = cache_k.shape[2]
    n_prompt = batch * seq
    n_latent = dec_batch * dec_seq

    cvec = jnp.concatenate([c_ctx[None, :], c, jnp.zeros((MOD_ROWS - 1 - dec_batch, d), F32)], axis=0)
    mod = _ada_table(cvec, w_ada, b_ada).reshape(depth, MOD_ROWS, N_MOD, 1, d)

    wg = _bf16_col_chunks(ffn_w_gate.reshape(depth * 2, d, d_ff), FFN_CHUNK)
    wu = _bf16_col_chunks(ffn_w_up.reshape(depth * 2, d, d_ff), FFN_CHUNK)
    wd = _bf16_row_pad(ffn_w_down.reshape(depth * 2, d_ff, d), FFN_CHUNK)
    w_in_b = _bf16_col_chunks(w_in, INPROJ_CHUNK)
    w_co_b = w_conv_out.astype(BF16)
    w_no_b = w_na_out.astype(BF16)
    w_o_b = w_out.astype(BF16)
    ck = cache_k.reshape(dec_batch, depth, past, d_na).astype(BF16)
    cv = cache_v.reshape(dec_batch, depth, past, d_na).astype(BF16)
    norm_g4 = norm_g.reshape(depth, 3, 1, d)

    a_cols = 2 * dc + d_na
    na_bias = _na_bias_table(rel_bias, dec_seq // GRID_W)

    xs = (x_prompt.reshape(n_prompt, d), x_sample.reshape(n_latent, d))
    keys, vals = [], []
    for l in range(depth):
        x = _ffn(xs, mod, norm_g4, wg, wu, wd, l, 0, 0, n_prompt, dec_seq)

        a, kv, gates = _inproj(x, mod, norm_g4, w_in_b, l, a_cols, 2 * d_na, 2 * d, n_prompt, dec_seq)
        keys.append(kv[:n_prompt, :d_na].reshape(batch, seq, N_HEADS, HEAD_DIM))
        vals.append(kv[:n_prompt, d_na:].reshape(batch, seq, N_HEADS, HEAD_DIM))

        na_p = _ctx_attention(a, kv, batch, seq, 2 * dc)
        na_s = _na_attention(a, kv, ck, cv, na_bias, l, n_prompt, dec_batch, dec_seq, 2 * dc)
        x = _merge_conv(x, mod, a, na_p, na_s, gates, conv_w, conv_b, conv_ln_g, conv_ln_b, w_co_b, w_no_b,
                        w_o_b, l, n_prompt, seq, dec_seq)

        out = _ffn((x,), mod, norm_g4, wg, wu, wd, l, 1, 6, n_prompt, dec_seq,
                   final_g=final_g if l == depth - 1 else None)
        xs = (out,)

    y_prompt, y_sample = out
    return (y_prompt.reshape(batch, seq, d), y_sample.reshape(dec_batch, dec_seq, d),
            jnp.stack(keys, axis=1), jnp.stack(vals, axis=1))
```
